```python
import math
import jax, jax.numpy as jnp
from jax import lax
import numpy as np

D_MODEL = 2048
BATCH = 2
SEQ = 8192
DEPTH = 2

CTX_LEN = 256
GRID_W = 64
HEAD_DIM = 128
ATTN_WIDTH = D_MODEL // 2
ATTN_HEADS = ATTN_WIDTH // HEAD_DIM
KV_HEADS = ATTN_HEADS // 4
GQA_GROUP = ATTN_HEADS // KV_HEADS
KV_WIDTH = KV_HEADS * HEAD_DIM
POOL_WIDTH = D_MODEL // 4
POOL_WINDOWS = (2, 4, 8, 16)
POOL_GROUP = POOL_WIDTH // len(POOL_WINDOWS)
CONV_WIDTH = D_MODEL // 4
CONV_K = 3
D_FF = 4 * D_MODEL
Q_BLOCK = 128
ROPE_BASE = 10000.0
EPS = 1e-6
ATTN_SCALE = 1.0 / math.sqrt(HEAD_DIM)

K_OFF = ATTN_WIDTH
V_OFF = K_OFF + KV_WIDTH
POOL_OFF = V_OFF + KV_WIDTH
CB_OFF = POOL_OFF + POOL_WIDTH
CC_OFF = CB_OFF + CONV_WIDTH
CV_OFF = CC_OFF + CONV_WIDTH
IN_WIDTH = CV_OFF + CONV_WIDTH
MIX_WIDTH = ATTN_WIDTH + POOL_WIDTH + CONV_WIDTH

kernel_name = "hybrid_pool_conv_gqa_diffusion_block"


def _rmsnorm(x, g):
    x32 = x.astype(jnp.float32)
    y = x32 * lax.rsqrt(jnp.mean(x32 * x32, axis=-1, keepdims=True) + EPS)
    return (y * g.astype(jnp.float32)).astype(x.dtype)


def _axial_rope(n):
    rows = n // GRID_W
    row = jnp.repeat(jnp.arange(rows, dtype=jnp.float32), GRID_W)
    col = jnp.tile(jnp.arange(GRID_W, dtype=jnp.float32), rows)
    n_freq = HEAD_DIM // 4
    inv = ROPE_BASE ** (-jnp.arange(n_freq, dtype=jnp.float32) / n_freq)
    ang_r = row[:, None] * inv
    ang_c = col[:, None] * inv
    ang = jnp.concatenate([ang_r, ang_r, ang_c, ang_c], axis=-1)
    return jnp.cos(ang), jnp.sin(ang)


def _rope(x, cos, sin):
    x1, x2, x3, x4 = jnp.split(x, 4, axis=-1)
    rot = jnp.concatenate([-x2, x1, -x4, x3], axis=-1)
    return (x * cos + rot * sin).astype(x.dtype)


def _attend(q, k, v):
    s = jnp.einsum('bqhgd,bkhd->bhgqk', q, k, preferred_element_type=jnp.float32) * ATTN_SCALE
    p = jax.nn.softmax(s, axis=-1)
    return jnp.einsum('bhgqk,bkhd->bqhgd', p.astype(v.dtype), v, preferred_element_type=jnp.float32)


def _latent_attention(q, k, v, kc, vc, qg, kg, cos, sin):
    b, n, _ = q.shape
    q = q.reshape(b, n, KV_HEADS, GQA_GROUP, HEAD_DIM)
    k = k.reshape(b, n, KV_HEADS, HEAD_DIM)
    v = v.reshape(b, n, KV_HEADS, HEAD_DIM)
    q = _rope(_rmsnorm(q, qg), cos[:, None, None, :], sin[:, None, None, :])
    k = _rope(_rmsnorm(k, kg), cos[:, None, :], sin[:, None, :])
    keys = jnp.concatenate([kc.astype(k.dtype), k], axis=1)
    vals = jnp.concatenate([vc.astype(v.dtype), v], axis=1)
    nb = n // Q_BLOCK
    qb = q.reshape(b, nb, Q_BLOCK, KV_HEADS, GQA_GROUP, HEAD_DIM).swapaxes(0, 1)
    out = lax.map(lambda qq: _attend(qq, keys, vals), qb)
    return out.swapaxes(0, 1).reshape(b, n, ATTN_WIDTH).astype(q.dtype)


def _context_attention(qc, kc, vc, qg):
    b, m, _ = qc.shape
    qc = _rmsnorm(qc.reshape(b, m, KV_HEADS, GQA_GROUP, HEAD_DIM), qg)
    return _attend(qc, kc, vc).reshape(b, m, ATTN_WIDTH).astype(qc.dtype)


def _pool_mixer(u, w_pool, pool_scale):
    b, n, _ = u.shape
    u32 = u.astype(jnp.float32)
    csum = jnp.concatenate([jnp.zeros((b, 1, POOL_WIDTH), jnp.float32), jnp.cumsum(u32, axis=1)], axis=1)
    t = jnp.arange(n)
    diffs = []
    for gi, w in enumerate(POOL_WINDOWS):
        lo = w // 2
        hi = w - 1 - lo
        start = jnp.clip(t - lo, 0, n)
        end = jnp.clip(t + hi + 1, 0, n)
        seg = csum[:, :, gi * POOL_GROUP:(gi + 1) * POOL_GROUP]
        cnt = (end - start).astype(jnp.float32)[None, :, None]
        diffs.append((seg[:, end] - seg[:, start]) / cnt - u32[..., gi * POOL_GROUP:(gi + 1) * POOL_GROUP])
    d = jnp.stack(diffs, axis=2)
    y = jnp.einsum('bngc,gce->bnge', d, w_pool.astype(jnp.float32)).reshape(b, n, POOL_WIDTH)
    return (y * pool_scale.astype(jnp.float32)).astype(u.dtype)


def _short_conv_mixer(gb, gc, v, conv_w):
    z = gc * v
    rhs = conv_w[:, None, :].astype(z.dtype)
    conv = lax.conv_general_dilated(z, rhs, window_strides=(1,), padding=[(CONV_K // 2, CONV_K // 2)],
                                    dimension_numbers=('NWC', 'WIO', 'NWC'), feature_group_count=CONV_WIDTH)
    return gb * conv


def _mix_out(att, pu, gb, gc, cv, w_pool, pool_scale, conv_w, w_out):
    pool = _pool_mixer(pu, w_pool, pool_scale)
    conv = _short_conv_mixer(gb, gc, cv, conv_w)
    cat = jnp.concatenate([att, pool.astype(att.dtype), conv.astype(att.dtype)], axis=-1)
    return jnp.einsum('bne,ed->bnd', cat, w_out)


def _sqrelu_mlp(h, w1, w2):
    u = jax.nn.relu(jnp.einsum('bnd,df->bnf', h, w1))
    return jnp.einsum('bnf,fd->bnd', u * u, w2)


def _split_in(p):
    return jnp.split(p, [K_OFF, V_OFF, POOL_OFF, CB_OFF, CC_OFF, CV_OFF], axis=-1)


def setup_inputs(seed: int = 0) -> dict:
    key = jax.random.key(seed)
    ks = jax.random.split(key, 20)
    f32 = jnp.float32
    nrm = lambda k, shape, s: jax.random.normal(k, shape, f32) * s
    return {
        "x": nrm(ks[0], (BATCH, SEQ, D_MODEL), 1.0),
        "c": nrm(ks[1], (BATCH, D_MODEL), 1.0),
        "ctx": nrm(ks[2], (BATCH, CTX_LEN, D_MODEL), 1.0),
        "c_ctx": nrm(ks[3], (D_MODEL,), 1.0),
        "w_mod": nrm(ks[4], (DEPTH, D_MODEL, 6 * D_MODEL), 0.5 * D_MODEL ** -0.5),
        "b_mod": nrm(ks[5], (DEPTH, 6 * D_MODEL), 0.02),
        "norm1_g": 1.0 + nrm(ks[6], (DEPTH, D_MODEL), 0.05),
        "norm2_g": 1.0 + nrm(ks[7], (DEPTH, D_MODEL), 0.05),
        "w_in": nrm(ks[8], (DEPTH, D_MODEL, IN_WIDTH), D_MODEL ** -0.5),
        "q_norm_g": 1.0 + nrm(ks[9], (DEPTH, HEAD_DIM), 0.05),
        "k_norm_g": 1.0 + nrm(ks[10], (DEPTH, HEAD_DIM), 0.05),
        "w_pool": nrm(ks[11], (DEPTH, len(POOL_WINDOWS), POOL_GROUP, POOL_GROUP), POOL_GROUP ** -0.5),
        "pool_scale": 1.0 + nrm(ks[12], (DEPTH, POOL_WIDTH), 0.1),
        "conv_w": nrm(ks[13], (DEPTH, CONV_K, CONV_WIDTH), CONV_K ** -0.5),
        "w_out": nrm(ks[14], (DEPTH, MIX_WIDTH, D_MODEL), MIX_WIDTH ** -0.5),
        "w_ff1": nrm(ks[15], (DEPTH, D_MODEL, D_FF), D_MODEL ** -0.5),
        "w_ff2": nrm(ks[16], (DEPTH, D_FF, D_MODEL), D_FF ** -0.5),
    }


def reference(x, c, ctx, c_ctx, w_mod, b_mod, norm1_g, norm2_g, w_in, q_norm_g, k_norm_g,
              w_pool, pool_scale, conv_w, w_out, w_ff1, w_ff2):
    b, n, _ = x.shape
    m = ctx.shape[1]
    cos, sin = _axial_rope(n)
    c_act = jax.nn.silu(c)
    cc_act = jax.nn.silu(c_ctx)
    xc = ctx
    for l in range(DEPTH):
        last = l == DEPTH - 1
        mod = jnp.einsum('bd,de->be', c_act, w_mod[l]) + b_mod[l]
        sh1, sc1, gt1, sh2, sc2, gt2 = jnp.split(mod[:, None, :], 6, axis=-1)
        modc = jnp.einsum('d,de->e', cc_act, w_mod[l]) + b_mod[l]
        csh1, csc1, cgt1, csh2, csc2, cgt2 = jnp.split(modc, 6)

        hc = _rmsnorm(xc, norm1_g[l]) * (1.0 + csc1) + csh1
        if last:
            kc, vc = jnp.split(jnp.einsum('bnd,de->bne', hc, w_in[l][:, K_OFF:POOL_OFF]), 2, axis=-1)
        else:
            qc, kc, vc, puc, cbc, ccc, cvc = _split_in(jnp.einsum('bnd,de->bne', hc, w_in[l]))
        kc = _rmsnorm(kc.reshape(b, m, KV_HEADS, HEAD_DIM), k_norm_g[l])
        vc = vc.reshape(b, m, KV_HEADS, HEAD_DIM)

        h = _rmsnorm(x, norm1_g[l]) * (1.0 + sc1) + sh1
        q, k, v, pu, cb, cc, cv = _split_in(jnp.einsum('bnd,de->bne', h, w_in[l]))
        att = _latent_attention(q, k, v, kc, vc, q_norm_g[l], k_norm_g[l], cos, sin)
        x = x + gt1 * _mix_out(att, pu, cb, cc, cv, w_pool[l], pool_scale[l], conv_w[l], w_out[l])
        h2 = _rmsnorm(x, norm2_g[l]) * (1.0 + sc2) + sh2
        x = x + gt2 * _sqrelu_mlp(h2, w_ff1[l], w_ff2[l])

        if not last:
            attc = _context_attention(qc, kc, vc, q_norm_g[l])
            xc = xc + cgt1 * _mix_out(attc, puc, cbc, ccc, cvc, w_pool[l], pool_scale[l], conv_w[l], w_out[l])
            hc2 = _rmsnorm(xc, norm2_g[l]) * (1.0 + csc2) + csh2
            xc = xc + cgt2 * _sqrelu_mlp(hc2, w_ff1[l], w_ff2[l])
    return x
```

```python
import functools
import math

import jax
import jax.numpy as jnp
from jax import lax
from jax.experimental import pallas as pl
from jax.experimental.pallas import tpu as pltpu

D_MODEL = 2048
GRID_W = 64
HEAD_DIM = 128
ATTN_WIDTH = D_MODEL // 2
ATTN_HEADS = ATTN_WIDTH // HEAD_DIM
KV_HEADS = ATTN_HEADS // 4
GQA_GROUP = ATTN_HEADS // KV_HEADS
KV_WIDTH = KV_HEADS * HEAD_DIM
POOL_WIDTH = D_MODEL // 4
POOL_WINDOWS = (2, 4, 8, 16)
POOL_GROUP = POOL_WIDTH // len(POOL_WINDOWS)
CONV_WIDTH = D_MODEL // 4
CONV_K = 3
D_FF = 4 * D_MODEL
ROPE_BASE = 10000.0
EPS = 1e-6
ATTN_SCALE = 1.0 / math.sqrt(HEAD_DIM)

K_OFF = ATTN_WIDTH
V_OFF = K_OFF + KV_WIDTH
POOL_OFF = V_OFF + KV_WIDTH
CB_OFF = POOL_OFF + POOL_WIDTH
CC_OFF = CB_OFF + CONV_WIDTH
CV_OFF = CC_OFF + CONV_WIDTH
IN_WIDTH = CV_OFF + CONV_WIDTH

HALO = 8
MOD_ROWS = 8
MIB = 1024 * 1024

F32 = jnp.float32
BF16 = jnp.bfloat16


def _resident(block_shape, index_map):
    return pl.BlockSpec(block_shape, index_map, pipeline_mode=pl.Buffered(1))


def _mod_kernel(a_ref, w_ref, b_ref, o_ref):
    a = a_ref[...]
    act = a / (1.0 + jnp.exp(-a))
    o_ref[...] = jnp.dot(act.astype(BF16), w_ref[...].astype(BF16),
                         preferred_element_type=F32) + b_ref[...]


def _modulation(cond, w_mod, b_mod):
    depth = w_mod.shape[0]
    tn = 1024
    return pl.pallas_call(
        _mod_kernel,
        grid=(depth, 6 * D_MODEL // tn),
        in_specs=[
            pl.BlockSpec((MOD_ROWS, D_MODEL), lambda l, j: (0, 0)),
            pl.BlockSpec((None, D_MODEL, tn), lambda l, j: (l, 0, j)),
            pl.BlockSpec((None, 1, tn), lambda l, j: (l, 0, j)),
        ],
        out_specs=pl.BlockSpec((None, MOD_ROWS, tn), lambda l, j: (l, 0, j)),
        out_shape=jax.ShapeDtypeStruct((depth, MOD_ROWS, 6 * D_MODEL), F32),
        compiler_params=pltpu.CompilerParams(
            dimension_semantics=("arbitrary", "arbitrary"), vmem_limit_bytes=40 * MIB),
        name="modulation",
    )(cond, w_mod, b_mod.reshape(depth, 1, 6 * D_MODEL))


def _head_rmsnorm(x, g):
    return x * lax.rsqrt(jnp.mean(x * x, axis=-1, keepdims=True) + EPS) * g


def _inproj_kernel(x_ref, mod_ref, g1_ref, w_ref, qg_ref, kg_ref, cos_ref, sa_ref, sb_ref,
                   q_ref, kt_ref, v_ref, pu_ref, cb_ref, z_ref, *, rope):
    x = x_ref[...]
    y = x * lax.rsqrt(jnp.mean(x * x, axis=-1, keepdims=True) + EPS) * g1_ref[...]
    sh = mod_ref[:, 0:D_MODEL]
    sc = mod_ref[:, D_MODEL:2 * D_MODEL]
    h = (y * (1.0 + sc) + sh).astype(BF16)

    def proj(lo, width):
        return jnp.dot(h, w_ref[:, lo:lo + width], preferred_element_type=F32)

    def rotary(t):
        if not rope:
            return t
        return (t * cos_ref[...] + pltpu.roll(t, 3 * HEAD_DIM // 4, axis=1) * sa_ref[...]
                + pltpu.roll(t, HEAD_DIM // 4, axis=1) * sb_ref[...])

    pq = proj(0, ATTN_WIDTH)
    for hh in range(ATTN_HEADS):
        t = _head_rmsnorm(pq[:, hh * HEAD_DIM:(hh + 1) * HEAD_DIM], qg_ref[...])
        q_ref[:, hh * HEAD_DIM:(hh + 1) * HEAD_DIM] = (rotary(t) * ATTN_SCALE).astype(BF16)

    pk = proj(K_OFF, KV_WIDTH)
    for hh in range(KV_HEADS):
        t = rotary(_head_rmsnorm(pk[:, hh * HEAD_DIM:(hh + 1) * HEAD_DIM], kg_ref[...]))
        kt_ref[hh * HEAD_DIM:(hh + 1) * HEAD_DIM, :] = t.T.astype(BF16)

    v_ref[...] = proj(V_OFF, KV_WIDTH).astype(BF16)
    pu_ref[...] = proj(POOL_OFF, POOL_WIDTH)
    cb_ref[...] = proj(CB_OFF, CONV_WIDTH)
    z_ref[...] = proj(CC_OFF, CONV_WIDTH) * proj(CV_OFF, CONV_WIDTH)


def _in_projection(x, mod, g1, w_in, qg, kg, cos, sa, sb, *, tm, rope):
    b, n, _ = x.shape
    row = lambda width: pl.BlockSpec((None, tm, width), lambda bi, i: (bi, i, 0))
    tab = pl.BlockSpec((tm, HEAD_DIM), lambda bi, i: (i, 0))
    vec = lambda width: pl.BlockSpec((1, width), lambda bi, i: (0, 0))
    out_shape = (
        jax.ShapeDtypeStruct((b, n, ATTN_WIDTH), BF16),
        jax.ShapeDtypeStruct((b, KV_WIDTH, n), BF16),
        jax.ShapeDtypeStruct((b, n, KV_WIDTH), BF16),
        jax.ShapeDtypeStruct((b, n, POOL_WIDTH), F32),
        jax.ShapeDtypeStruct((b, n, CONV_WIDTH), F32),
        jax.ShapeDtypeStruct((b, n, CONV_WIDTH), F32),
    )
    return pl.pallas_call(
        functools.partial(_inproj_kernel, rope=rope),
        grid=(b, n // tm),
        in_specs=[
            row(D_MODEL),
            pl.BlockSpec((None, 1, 6 * D_MODEL), lambda bi, i: (bi, 0, 0)),
            vec(D_MODEL),
            _resident((D_MODEL, IN_WIDTH), lambda bi, i: (0, 0)),
            vec(HEAD_DIM), vec(HEAD_DIM),
            tab, tab, tab,
        ],
        out_specs=(
            row(ATTN_WIDTH),
            pl.BlockSpec((None, KV_WIDTH, tm), lambda bi, i: (bi, 0, i)),
            row(KV_WIDTH), row(POOL_WIDTH), row(CONV_WIDTH), row(CONV_WIDTH),
        ),
        out_shape=out_shape,
        compiler_params=pltpu.CompilerParams(
            dimension_semantics=("arbitrary", "arbitrary"), vmem_limit_bytes=56 * MIB),
        name="in_projection",
    )(x, mod, g1, w_in, qg, kg, cos, sa, sb)


def _attn_kernel(q_ref, kct_ref, vc_ref, kt_ref, v_ref, o_ref, *, tq, tk, n_chunks):
    qs = jnp.concatenate([q_ref[:, g * HEAD_DIM:(g + 1) * HEAD_DIM] for g in range(GQA_GROUP)], axis=0)

    s = jnp.dot(qs, kct_ref[...], preferred_element_type=F32)
    m = jnp.max(s, axis=-1, keepdims=True)
    p = jnp.exp(s - m)
    l = jnp.sum(p, axis=-1, keepdims=True)
    acc = jnp.dot(p.astype(BF16), vc_ref[...], preferred_element_type=F32)

    def body(c, carry):
        m, l, acc = carry
        off = pl.multiple_of(c * tk, tk)
        s = jnp.dot(qs, kt_ref[:, pl.ds(off, tk)], preferred_element_type=F32)
        m_new = jnp.maximum(m, jnp.max(s, axis=-1, keepdims=True))
        alpha = jnp.exp(m - m_new)
        p = jnp.exp(s - m_new)
        l = alpha * l + jnp.sum(p, axis=-1, keepdims=True)
        acc = alpha * acc + jnp.dot(p.astype(BF16), v_ref[pl.ds(off, tk), :], preferred_element_type=F32)
        return m_new, l, acc

    if n_chunks:
        m, l, acc = lax.fori_loop(0, n_chunks, body, (m, l, acc))
    out = (acc / l).astype(o_ref.dtype)
    for g in range(GQA_GROUP):
        o_ref[:, g * HEAD_DIM:(g + 1) * HEAD_DIM] = out[g * tq:(g + 1) * tq]


def _attention(q, kct, vc, kt, v, *, tq, tk):
    b, n, _ = q.shape
    m = vc.shape[1]
    has_lat = kt is not None
    if not has_lat:
        kt, v = kct, vc
    nk = v.shape[1]
    n_chunks = nk // tk if has_lat else 0
    grp = GQA_GROUP * HEAD_DIM
    return pl.pallas_call(
        functools.partial(_attn_kernel, tq=tq, tk=tk, n_chunks=n_chunks),
        grid=(b, KV_HEADS, n // tq),
        in_specs=[
            pl.BlockSpec((None, tq, grp), lambda bi, h, i: (bi, i, h)),
            pl.BlockSpec((None, HEAD_DIM, m), lambda bi, h, i: (bi, h, 0)),
            pl.BlockSpec((None, m, HEAD_DIM), lambda bi, h, i: (bi, 0, h)),
            pl.BlockSpec((None, HEAD_DIM, nk), lambda bi, h, i: (bi, h, 0)),
            pl.BlockSpec((None, nk, HEAD_DIM), lambda bi, h, i: (bi, 0, h)),
        ],
        out_specs=pl.BlockSpec((None, tq, grp), lambda bi, h, i: (bi, i, h)),
        out_shape=jax.ShapeDtypeStruct((b, n, ATTN_WIDTH), BF16),
        compiler_params=pltpu.CompilerParams(
            dimension_semantics=("arbitrary", "arbitrary", "arbitrary"), vmem_limit_bytes=48 * MIB),
        name="attention",
    )(q, kct, vc, kt, v)


def _mixout_kernel(x_ref, att_ref, pu_ref, pu_prev_ref, pu_next_ref, z_ref, z_prev_ref, z_next_ref, cb_ref,
                   mod_ref, g2_ref, wout_ref, wpool_ref, pscale_ref, convw_ref,
                   x1_ref, h2_ref, epu_ref, ez_ref, cat_ref, *, tm, n):
    j = pl.program_id(1)
    first = j == 0
    last = j == pl.num_programs(1) - 1

    epu_ref[0:HALO, :] = jnp.where(first, 0.0, pu_prev_ref[...])
    epu_ref[HALO:HALO + tm, :] = pu_ref[...]
    epu_ref[HALO + tm:, :] = jnp.where(last, 0.0, pu_next_ref[...])
    ez_ref[0:HALO, :] = jnp.where(first, 0.0, z_prev_ref[...])
    ez_ref[HALO:HALO + tm, :] = z_ref[...]
    ez_ref[HALO + tm:, :] = jnp.where(last, 0.0, z_next_ref[...])

    cat_ref[:, 0:ATTN_WIDTH] = att_ref[...]

    pos = j * tm + lax.broadcasted_iota(jnp.int32, (tm, 1), 0)
    for gi, w in enumerate(POOL_WINDOWS):
        lo = w // 2
        hi = w - 1 - lo
        cols = slice(gi * POOL_GROUP, (gi + 1) * POOL_GROUP)
        tot = epu_ref[HALO - lo:HALO - lo + tm, cols]
        for o in range(-lo + 1, hi + 1):
            tot = tot + epu_ref[HALO + o:HALO + o + tm, cols]
        cnt = (jnp.minimum(pos + hi + 1, n) - jnp.maximum(pos - lo, 0)).astype(F32)
        d = tot / cnt - pu_ref[:, cols]
        yg = jnp.dot(d.astype(BF16), wpool_ref[gi], preferred_element_type=F32) * pscale_ref[:, cols]
        cat_ref[:, ATTN_WIDTH + gi * POOL_GROUP:ATTN_WIDTH + (gi + 1) * POOL_GROUP] = yg.astype(BF16)

    conv = (ez_ref[HALO - 1:HALO - 1 + tm, :] * convw_ref[0:1, :]
            + z_ref[...] * convw_ref[1:2, :]
            + ez_ref[HALO + 1:HALO + 1 + tm, :] * convw_ref[2:3, :])
    cat_ref[:, ATTN_WIDTH + POOL_WIDTH:] = (cb_ref[...] * conv).astype(BF16)

    gt1 = mod_ref[:, 2 * D_MODEL:3 * D_MODEL]
    sh2 = mod_ref[:, 3 * D_MODEL:4 * D_MODEL]
    sc2 = mod_ref[:, 4 * D_MODEL:5 * D_MODEL]
    x1 = x_ref[...] + gt1 * jnp.dot(cat_ref[...], wout_ref[...], preferred_element_type=F32)
    x1_ref[...] = x1
    y = x1 * lax.rsqrt(jnp.mean(x1 * x1, axis=-1, keepdims=True) + EPS) * g2_ref[...]
    h2_ref[...] = (y * (1.0 + sc2) + sh2).astype(BF16)


def _mix_out(x, att, pu, z, cb, mod, g2, w_out, w_pool, pool_scale, conv_w, *, tm):
    b, n, _ = x.shape
    hb = tm // HALO
    nhb = n // HALO
    row = lambda width: pl.BlockSpec((None, tm, width), lambda bi, i: (bi, i, 0))
    prev = lambda width: pl.BlockSpec((None, HALO, width), lambda bi, i: (bi, jnp.maximum(i * hb - 1, 0), 0))
    nxt = lambda width: pl.BlockSpec((None, HALO, width), lambda bi, i: (bi, jnp.minimum((i + 1) * hb, nhb - 1), 0))
    vec = lambda width: pl.BlockSpec((1, width), lambda bi, i: (0, 0))
    return pl.pallas_call(
        functools.partial(_mixout_kernel, tm=tm, n=n),
        grid=(b, n // tm),
        in_specs=[
            row(D_MODEL), row(ATTN_WIDTH),
            row(POOL_WIDTH), prev(POOL_WIDTH), nxt(POOL_WIDTH),
            row(CONV_WIDTH), prev(CONV_WIDTH), nxt(CONV_WIDTH),
            row(CONV_WIDTH),
            pl.BlockSpec((None, 1, 6 * D_MODEL), lambda bi, i: (bi, 0, 0)),
            vec(D_MODEL),
            _resident((D_MODEL, D_MODEL), lambda bi, i: (0, 0)),
            _resident((len(POOL_WINDOWS), POOL_GROUP, POOL_GROUP), lambda bi, i: (0, 0, 0)),
            vec(POOL_WIDTH),
            pl.BlockSpec((CONV_K, CONV_WIDTH), lambda bi, i: (0, 0)),
        ],
        out_specs=(row(D_MODEL), row(D_MODEL)),
        out_shape=(jax.ShapeDtypeStruct((b, n, D_MODEL), F32), jax.ShapeDtypeStruct((b, n, D_MODEL), BF16)),
        scratch_shapes=[
            pltpu.VMEM((tm + 2 * HALO, POOL_WIDTH), F32),
            pltpu.VMEM((tm + 2 * HALO, CONV_WIDTH), F32),
            pltpu.VMEM((tm, D_MODEL), BF16),
        ],
        compiler_params=pltpu.CompilerParams(
            dimension_semantics=("arbitrary", "arbitrary"), vmem_limit_bytes=56 * MIB),
        name="mix_out",
    )(x, att, pu, pu, pu, z, z, z, cb, mod, g2, w_out, w_pool, pool_scale, conv_w)


def _mlp_kernel(h_ref, x_ref, mod_ref, w1_ref, w2_ref, o_ref):
    f = pl.program_id(2)
    nf = pl.num_programs(2)
    u = jnp.maximum(jnp.dot(h_ref[...], w1_ref[...], preferred_element_type=F32), 0.0)
    part = jnp.dot((u * u).astype(BF16), w2_ref[...], preferred_element_type=F32)

    @pl.when(f == 0)
    def _():
        o_ref[...] = part

    @pl.when(jnp.logical_and(f > 0, f < nf - 1))
    def _():
        o_ref[...] += part

    @pl.when(f == nf - 1)
    def _():
        gt2 = mod_ref[:, 5 * D_MODEL:6 * D_MODEL]
        o_ref[...] = x_ref[...] + gt2 * (o_ref[...] + part)


def _mlp(h2, x1, mod, w1, w2, *, tm, tf):
    b, n, _ = x1.shape
    return pl.pallas_call(
        _mlp_kernel,
        grid=(b, n // tm, D_FF // tf),
        in_specs=[
            pl.BlockSpec((None, tm, D_MODEL), lambda bi, i, f: (bi, i, 0)),
            pl.BlockSpec((None, tm, D_MODEL), lambda bi, i, f: (bi, i, 0)),
            pl.BlockSpec((None, 1, 6 * D_MODEL), lambda bi, i, f: (bi, 0, 0)),
            pl.BlockSpec((D_MODEL, tf), lambda bi, i, f: (0, f)),
            pl.BlockSpec((tf, D_MODEL), lambda bi, i, f: (f, 0)),
        ],
        out_specs=pl.BlockSpec((None, tm, D_MODEL), lambda bi, i, f: (bi, i, 0)),
        out_shape=jax.ShapeDtypeStruct((b, n, D_MODEL), F32),
        compiler_params=pltpu.CompilerParams(
            dimension_semantics=("arbitrary", "arbitrary", "arbitrary"), vmem_limit_bytes=56 * MIB),
        name="mlp",
    )(h2, x1, mod, w1, w2)


def _rope_tables(n):
    rows = n // GRID_W
    row = jnp.repeat(jnp.arange(rows, dtype=F32), GRID_W)
    col = jnp.tile(jnp.arange(GRID_W, dtype=F32), rows)
    n_freq = HEAD_DIM // 4
    inv = ROPE_BASE ** (-jnp.arange(n_freq, dtype=F32) / n_freq)
    ang_r = row[:, None] * inv
    ang_c = col[:, None] * inv
    ang = jnp.concatenate([ang_r, ang_r, ang_c, ang_c], axis=-1)
    cos, sin = jnp.cos(ang), jnp.sin(ang)
    odd_quarter = (jnp.arange(HEAD_DIM) // (HEAD_DIM // 4)) % 2 == 1
    sa = jnp.where(odd_quarter, 0.0, -sin)
    sb = jnp.where(odd_quarter, sin, 0.0)
    return cos, sa, sb


def kernel(x, c, ctx, c_ctx, w_mod, b_mod, norm1_g, norm2_g, w_in, q_norm_g, k_norm_g, w_pool, pool_scale,
           conv_w, w_out, w_ff1, w_ff2):
    b, n, _ = x.shape
    m = ctx.shape[1]
    depth = w_mod.shape[0]
    assert b + 1 <= MOD_ROWS

    cos, sa, sb = _rope_tables(n)
    cond = jnp.concatenate([c, c_ctx[None, :], jnp.zeros((MOD_ROWS - b - 1, D_MODEL), F32)], axis=0)
    mod_all = _modulation(cond, w_mod, b_mod)

    w_in_b = w_in.astype(BF16)
    w_out_b = w_out.astype(BF16)
    w_pool_b = w_pool.astype(BF16)
    w1_b = w_ff1.astype(BF16)
    w2_b = w_ff2.astype(BF16)

    tm_lat, tm_ctx = 512, m
    xc = ctx
    for l in range(depth):
        last = l == depth - 1
        mod = mod_all[l, :b][:, None, :]
        modc = jnp.broadcast_to(mod_all[l, b][None, None, :], (b, 1, 6 * D_MODEL))
        g1 = norm1_g[l][None, :]
        g2 = norm2_g[l][None, :]
        qg = q_norm_g[l][None, :]
        kg = k_norm_g[l][None, :]
        ps = pool_scale[l][None, :]

        qc, kct, vc, puc, cbc, zc = _in_projection(xc, modc, g1, w_in_b[l], qg, kg, cos, sa, sb,
                                                   tm=tm_ctx, rope=False)
        q, kt, v, pu, cb, z = _in_projection(x, mod, g1, w_in_b[l], qg, kg, cos, sa, sb, tm=tm_lat, rope=True)
        att = _attention(q, kct, vc, kt, v, tq=128, tk=512)
        x1, h2 = _mix_out(x, att, pu, z, cb, mod, g2, w_out_b[l], w_pool_b[l], ps, conv_w[l], tm=tm_lat)
        x = _mlp(h2, x1, mod, w1_b[l], w2_b[l], tm=512, tf=1024)

        if not last:
            attc = _attention(qc, kct, vc, None, None, tq=128, tk=512)
            xc1, hc2 = _mix_out(xc, attc, puc, zc, cbc, modc, g2, w_out_b[l], w_pool_b[l], ps, conv_w[l],
                                tm=tm_ctx)
            xc = _mlp(hc2, xc1, modc, w1_b[l], w2_b[l], tm=tm_ctx, tf=1024)
    return x
```

```python
import functools
import math

import jax
import jax.numpy as jnp
from jax import lax
from jax.experimental import pallas as pl
from jax.experimental.pallas import tpu as pltpu

D_MODEL = 2048
GRID_W = 64
HEAD_DIM = 128
ATTN_WIDTH = D_MODEL // 2
ATTN_HEADS = ATTN_WIDTH // HEAD_DIM
KV_HEADS = ATTN_HEADS // 4
GQA_GROUP = ATTN_HEADS // KV_HEADS
KV_WIDTH = KV_HEADS * HEAD_DIM
POOL_WIDTH = D_MODEL // 4
POOL_WINDOWS = (2, 4, 8, 16)
POOL_GROUP = POOL_WIDTH // len(POOL_WINDOWS)
CONV_WIDTH = D_MODEL // 4
CONV_K = 3
D_FF = 4 * D_MODEL
ROPE_BASE = 10000.0
EPS = 1e-6
ATTN_SCALE = 1.0 / math.sqrt(HEAD_DIM)
Q_SCALE_LOG2 = ATTN_SCALE * math.log2(math.e)

K_OFF = ATTN_WIDTH
V_OFF = K_OFF + KV_WIDTH
POOL_OFF = V_OFF + KV_WIDTH
CB_OFF = POOL_OFF + POOL_WIDTH
CC_OFF = CB_OFF + CONV_WIDTH
CV_OFF = CC_OFF + CONV_WIDTH
IN_WIDTH = CV_OFF + CONV_WIDTH

HALO = 8
MOD_ROWS = 8
MIB = 1024 * 1024

F32 = jnp.float32
BF16 = jnp.bfloat16


def _resident(block_shape, index_map):
    return pl.BlockSpec(block_shape, index_map, pipeline_mode=pl.Buffered(1))


def _mod_kernel(a_ref, w_ref, b_ref, o_ref):
    a = a_ref[...]
    act = a / (1.0 + jnp.exp(-a))
    o_ref[...] = jnp.dot(act.astype(BF16), w_ref[...].astype(BF16),
                         preferred_element_type=F32) + b_ref[...]


def _modulation(cond, w_mod, b_mod):
    depth = w_mod.shape[0]
    tn = 1024
    return pl.pallas_call(
        _mod_kernel,
        grid=(depth, 6 * D_MODEL // tn),
        in_specs=[
            pl.BlockSpec((MOD_ROWS, D_MODEL), lambda l, j: (0, 0)),
            pl.BlockSpec((None, D_MODEL, tn), lambda l, j: (l, 0, j)),
            pl.BlockSpec((None, 1, tn), lambda l, j: (l, 0, j)),
        ],
        out_specs=pl.BlockSpec((None, MOD_ROWS, tn), lambda l, j: (l, 0, j)),
        out_shape=jax.ShapeDtypeStruct((depth, MOD_ROWS, 6 * D_MODEL), F32),
        compiler_params=pltpu.CompilerParams(
            dimension_semantics=("arbitrary", "arbitrary"), vmem_limit_bytes=40 * MIB),
        name="modulation",
    )(cond, w_mod, b_mod.reshape(depth, 1, 6 * D_MODEL))


def _head_rmsnorm(x, g):
    return x * lax.rsqrt(jnp.mean(x * x, axis=-1, keepdims=True) + EPS) * g


def _inproj_kernel(x_ref, mod_ref, g1_ref, w_ref, qg_ref, kg_ref, cos_ref, sa_ref, sb_ref,
                   q_ref, kt_ref, v_ref, pu_ref, cb_ref, z_ref, *, rope):
    x = x_ref[...]
    y = x * lax.rsqrt(jnp.mean(x * x, axis=-1, keepdims=True) + EPS) * g1_ref[...]
    sh = mod_ref[:, 0:D_MODEL]
    sc = mod_ref[:, D_MODEL:2 * D_MODEL]
    h = (y * (1.0 + sc) + sh).astype(BF16)

    def proj(lo, width):
        return jnp.dot(h, w_ref[:, lo:lo + width], preferred_element_type=F32)

    def rotary(t):
        if not rope:
            return t
        return (t * cos_ref[...] + pltpu.roll(t, 3 * HEAD_DIM // 4, axis=1) * sa_ref[...]
                + pltpu.roll(t, HEAD_DIM // 4, axis=1) * sb_ref[...])

    pq = proj(0, ATTN_WIDTH)
    for hh in range(ATTN_HEADS):
        t = _head_rmsnorm(pq[:, hh * HEAD_DIM:(hh + 1) * HEAD_DIM], qg_ref[...])
        q_ref[:, hh * HEAD_DIM:(hh + 1) * HEAD_DIM] = (rotary(t) * Q_SCALE_LOG2).astype(BF16)

    pk = proj(K_OFF, KV_WIDTH)
    for hh in range(KV_HEADS):
        t = rotary(_head_rmsnorm(pk[:, hh * HEAD_DIM:(hh + 1) * HEAD_DIM], kg_ref[...]))
        kt_ref[hh * HEAD_DIM:(hh + 1) * HEAD_DIM, :] = t.T.astype(BF16)

    v_ref[...] = proj(V_OFF, KV_WIDTH).astype(BF16)
    pu_ref[...] = proj(POOL_OFF, POOL_WIDTH)
    cb_ref[...] = proj(CB_OFF, CONV_WIDTH)
    z_ref[...] = proj(CC_OFF, CONV_WIDTH) * proj(CV_OFF, CONV_WIDTH)


def _in_projection(x, mod, g1, w_in, qg, kg, cos, sa, sb, *, layer, tm, rope):
    b, n, _ = x.shape
    row = lambda width: pl.BlockSpec((None, tm, width), lambda bi, i: (bi, i, 0))
    tab = pl.BlockSpec((tm, HEAD_DIM), lambda bi, i: (i, 0))
    vec = lambda width: pl.BlockSpec((1, width), lambda bi, i: (0, 0))
    out_shape = (
        jax.ShapeDtypeStruct((b, n, ATTN_WIDTH), BF16),
        jax.ShapeDtypeStruct((b, KV_WIDTH, n), BF16),
        jax.ShapeDtypeStruct((b, n, KV_WIDTH), BF16),
        jax.ShapeDtypeStruct((b, n, POOL_WIDTH), F32),
        jax.ShapeDtypeStruct((b, n, CONV_WIDTH), F32),
        jax.ShapeDtypeStruct((b, n, CONV_WIDTH), F32),
    )
    return pl.pallas_call(
        functools.partial(_inproj_kernel, rope=rope),
        grid=(b, n // tm),
        in_specs=[
            row(D_MODEL),
            pl.BlockSpec((None, 1, 6 * D_MODEL), lambda bi, i: (bi, 0, 0)),
            vec(D_MODEL),
            _resident((None, D_MODEL, IN_WIDTH), lambda bi, i: (layer, 0, 0)),
            vec(HEAD_DIM), vec(HEAD_DIM),
            tab, tab, tab,
        ],
        out_specs=(
            row(ATTN_WIDTH),
            pl.BlockSpec((None, KV_WIDTH, tm), lambda bi, i: (bi, 0, i)),
            row(KV_WIDTH), row(POOL_WIDTH), row(CONV_WIDTH), row(CONV_WIDTH),
        ),
        out_shape=out_shape,
        compiler_params=pltpu.CompilerParams(
            dimension_semantics=("arbitrary", "arbitrary"), vmem_limit_bytes=56 * MIB),
        name="in_projection",
    )(x, mod, g1, w_in, qg, kg, cos, sa, sb)


def _attn_kernel(q_ref, kt_ref, vx_ref, o_ref, s_ref, m_ref, acc_ref, *, tq, tk, n_chunks):
    qs = jnp.concatenate([q_ref[:, g * HEAD_DIM:(g + 1) * HEAD_DIM] for g in range(GQA_GROUP)], axis=0)

    def scores(c, slot):
        s_ref[slot] = jnp.dot(qs, kt_ref[:, pl.ds(c * tk, tk)], preferred_element_type=F32)

    def update(c, slot):
        s = s_ref[slot]
        m = m_ref[...]
        m_new = jnp.maximum(m, jnp.max(s, axis=-1, keepdims=True))
        p = jnp.exp2(s - m_new).astype(BF16)
        pv = jnp.dot(p, vx_ref[pl.ds(c * tk, tk), :], preferred_element_type=F32)
        acc_ref[...] = jnp.exp2(m - m_new) * acc_ref[...] + pv
        m_ref[...] = m_new

    m_ref[...] = jnp.full(m_ref.shape, -jnp.inf, F32)
    acc_ref[...] = jnp.zeros(acc_ref.shape, F32)
    scores(0, 0)

    def pair(i, carry):
        c = 2 * i
        scores(c + 1, 1)
        update(c, 0)
        scores(c + 2, 0)
        update(c + 1, 1)
        return carry

    n_pairs = (n_chunks - 1) // 2
    if n_pairs:
        lax.fori_loop(0, n_pairs, pair, 0)
    if n_chunks - 1 - 2 * n_pairs:
        scores(n_chunks - 1, 1)
        update(n_chunks - 2, 0)
        update(n_chunks - 1, 1)
    else:
        update(n_chunks - 1, 0)
    acc = acc_ref[...]
    out = (acc[:, :HEAD_DIM] / acc[:, HEAD_DIM:]).astype(o_ref.dtype)
    for g in range(GQA_GROUP):
        o_ref[:, g * HEAD_DIM:(g + 1) * HEAD_DIM] = out[g * tq:(g + 1) * tq]


def _attention(q, kt, vx, *, tq, tk):
    b, n, _ = q.shape
    nk = kt.shape[2]
    grp = GQA_GROUP * HEAD_DIM
    return pl.pallas_call(
        functools.partial(_attn_kernel, tq=tq, tk=tk, n_chunks=nk // tk),
        grid=(b, KV_HEADS, n // tq),
        in_specs=[
            pl.BlockSpec((None, tq, grp), lambda bi, h, i: (bi, i, h)),
            pl.BlockSpec((None, HEAD_DIM, nk), lambda bi, h, i: (bi, h, 0)),
            pl.BlockSpec((None, nk, 2 * HEAD_DIM), lambda bi, h, i: (bi, 0, h)),
        ],
        out_specs=pl.BlockSpec((None, tq, grp), lambda bi, h, i: (bi, i, h)),
        out_shape=jax.ShapeDtypeStruct((b, n, ATTN_WIDTH), BF16),
        scratch_shapes=[
            pltpu.VMEM((2, GQA_GROUP * tq, tk), F32),
            pltpu.VMEM((GQA_GROUP * tq, 1), F32),
            pltpu.VMEM((GQA_GROUP * tq, 2 * HEAD_DIM), F32),
        ],
        compiler_params=pltpu.CompilerParams(
            dimension_semantics=("arbitrary", "arbitrary", "arbitrary"), vmem_limit_bytes=48 * MIB),
        name="attention",
    )(q, kt, vx)


def _with_ones(v):
    b, nk, _ = v.shape
    v4 = v.reshape(b, nk, KV_HEADS, HEAD_DIM)
    return jnp.concatenate([v4, jnp.ones_like(v4)], axis=-1).reshape(b, nk, 2 * KV_WIDTH)


def _mixout_kernel(x_ref, att_ref, pu_ref, pu_prev_ref, pu_next_ref, z_ref, z_prev_ref, z_next_ref, cb_ref,
                   mod_ref, g2_ref, wout_ref, wpool_ref, pscale_ref, convw_ref,
                   x1_ref, h2_ref, epu_ref, ez_ref, cat_ref, *, tm, n):
    j = pl.program_id(1)
    first = j == 0
    last = j == pl.num_programs(1) - 1

    epu_ref[0:HALO, :] = jnp.where(first, 0.0, pu_prev_ref[...])
    epu_ref[HALO:HALO + tm, :] = pu_ref[...]
    epu_ref[HALO + tm:, :] = jnp.where(last, 0.0, pu_next_ref[...])
    ez_ref[0:HALO, :] = jnp.where(first, 0.0, z_prev_ref[...])
    ez_ref[HALO:HALO + tm, :] = z_ref[...]
    ez_ref[HALO + tm:, :] = jnp.where(last, 0.0, z_next_ref[...])

    cat_ref[:, 0:ATTN_WIDTH] = att_ref[...]

    pos = j * tm + lax.broadcasted_iota(jnp.int32, (tm, 1), 0)
    for gi, w in enumerate(POOL_WINDOWS):
        lo = w // 2
        hi = w - 1 - lo
        cols = slice(gi * POOL_GROUP, (gi + 1) * POOL_GROUP)
        tot = epu_ref[HALO - lo:HALO - lo + tm, cols]
        for o in range(-lo + 1, hi + 1):
            tot = tot + epu_ref[HALO + o:HALO + o + tm, cols]
        cnt = (jnp.minimum(pos + hi + 1, n) - jnp.maximum(pos - lo, 0)).astype(F32)
        d = tot / cnt - pu_ref[:, cols]
        yg = jnp.dot(d.astype(BF16), wpool_ref[gi], preferred_element_type=F32) * pscale_ref[:, cols]
        cat_ref[:, ATTN_WIDTH + gi * POOL_GROUP:ATTN_WIDTH + (gi + 1) * POOL_GROUP] = yg.astype(BF16)

    conv = (ez_ref[HALO - 1:HALO - 1 + tm, :] * convw_ref[0:1, :]
            + z_ref[...] * convw_ref[1:2, :]
            + ez_ref[HALO + 1:HALO + 1 + tm, :] * convw_ref[2:3, :])
    cat_ref[:, ATTN_WIDTH + POOL_WIDTH:] = (cb_ref[...] * conv).astype(BF16)

    gt1 = mod_ref[:, 2 * D_MODEL:3 * D_MODEL]
    sh2 = mod_ref[:, 3 * D_MODEL:4 * D_MODEL]
    sc2 = mod_ref[:, 4 * D_MODEL:5 * D_MODEL]
    x1 = x_ref[...] + gt1 * jnp.dot(cat_ref[...], wout_ref[...], preferred_element_type=F32)
    x1_ref[...] = x1
    y = x1 * lax.rsqrt(jnp.mean(x1 * x1, axis=-1, keepdims=True) + EPS) * g2_ref[...]
    h2_ref[...] = (y * (1.0 + sc2) + sh2).astype(BF16)


def _mix_out(x, att, pu, z, cb, mod, g2, w_out, w_pool, pool_scale, conv_w, *, layer, tm):
    b, n, _ = x.shape
    hb = tm // HALO
    nhb = n // HALO
    row = lambda width: pl.BlockSpec((None, tm, width), lambda bi, i: (bi, i, 0))
    prev = lambda width: pl.BlockSpec((None, HALO, width), lambda bi, i: (bi, jnp.maximum(i * hb - 1, 0), 0))
    nxt = lambda width: pl.BlockSpec((None, HALO, width), lambda bi, i: (bi, jnp.minimum((i + 1) * hb, nhb - 1), 0))
    vec = lambda width: pl.BlockSpec((1, width), lambda bi, i: (0, 0))
    return pl.pallas_call(
        functools.partial(_mixout_kernel, tm=tm, n=n),
        grid=(b, n // tm),
        in_specs=[
            row(D_MODEL), row(ATTN_WIDTH),
            row(POOL_WIDTH), prev(POOL_WIDTH), nxt(POOL_WIDTH),
            row(CONV_WIDTH), prev(CONV_WIDTH), nxt(CONV_WIDTH),
            row(CONV_WIDTH),
            pl.BlockSpec((None, 1, 6 * D_MODEL), lambda bi, i: (bi, 0, 0)),
            vec(D_MODEL),
            _resident((None, D_MODEL, D_MODEL), lambda bi, i: (layer, 0, 0)),
            _resident((None, len(POOL_WINDOWS), POOL_GROUP, POOL_GROUP), lambda bi, i: (layer, 0, 0, 0)),
            vec(POOL_WIDTH),
            pl.BlockSpec((CONV_K, CONV_WIDTH), lambda bi, i: (0, 0)),
        ],
        out_specs=(row(D_MODEL), row(D_MODEL)),
        out_shape=(jax.ShapeDtypeStruct((b, n, D_MODEL), F32), jax.ShapeDtypeStruct((b, n, D_MODEL), BF16)),
        scratch_shapes=[
            pltpu.VMEM((tm + 2 * HALO, POOL_WIDTH), F32),
            pltpu.VMEM((tm + 2 * HALO, CONV_WIDTH), F32),
            pltpu.VMEM((tm, D_MODEL), BF16),
        ],
        compiler_params=pltpu.CompilerParams(
            dimension_semantics=("arbitrary", "arbitrary"), vmem_limit_bytes=56 * MIB),
        name="mix_out",
    )(x, att, pu, pu, pu, z, z, z, cb, mod, g2, w_out, w_pool, pool_scale, conv_w)


def _mlp_kernel(h_ref, x_ref, mod_ref, w1_ref, w2_ref, o_ref):
    f = pl.program_id(2)
    nf = pl.num_programs(2)
    u = jnp.maximum(jnp.dot(h_ref[...], w1_ref[...], preferred_element_type=F32), 0.0)
    part = jnp.dot((u * u).astype(BF16), w2_ref[...], preferred_element_type=F32)

    @pl.when(f == 0)
    def _():
        o_ref[...] = part

    @pl.when(jnp.logical_and(f > 0, f < nf - 1))
    def _():
        o_ref[...] += part

    @pl.when(f == nf - 1)
    def _():
        gt2 = mod_ref[:, 5 * D_MODEL:6 * D_MODEL]
        o_ref[...] = x_ref[...] + gt2 * (o_ref[...] + part)


def _mlp(h2, x1, mod, w1, w2, *, layer, tm, tf):
    b, n, _ = x1.shape
    return pl.pallas_call(
        _mlp_kernel,
        grid=(b, n // tm, D_FF // tf),
        in_specs=[
            pl.BlockSpec((None, tm, D_MODEL), lambda bi, i, f: (bi, i, 0)),
            pl.BlockSpec((None, tm, D_MODEL), lambda bi, i, f: (bi, i, 0)),
            pl.BlockSpec((None, 1, 6 * D_MODEL), lambda bi, i, f: (bi, 0, 0)),
            pl.BlockSpec((None, D_MODEL, tf), lambda bi, i, f: (layer, 0, f)),
            pl.BlockSpec((None, tf, D_MODEL), lambda bi, i, f: (layer, f, 0)),
        ],
        out_specs=pl.BlockSpec((None, tm, D_MODEL), lambda bi, i, f: (bi, i, 0)),
        out_shape=jax.ShapeDtypeStruct((b, n, D_MODEL), F32),
        compiler_params=pltpu.CompilerParams(
            dimension_semantics=("arbitrary", "arbitrary", "arbitrary"), vmem_limit_bytes=56 * MIB),
        name="mlp",
    )(h2, x1, mod, w1, w2)


def _rope_tables(n):
    rows = n // GRID_W
    row = jnp.repeat(jnp.arange(rows, dtype=F32), GRID_W)
    col = jnp.tile(jnp.arange(GRID_W, dtype=F32), rows)
    n_freq = HEAD_DIM // 4
    inv = ROPE_BASE ** (-jnp.arange(n_freq, dtype=F32) / n_freq)
    ang_r = row[:, None] * inv
    ang_c = col[:, None] * inv
    ang = jnp.concatenate([ang_r, ang_r, ang_c, ang_c], axis=-1)
    cos, sin = jnp.cos(ang), jnp.sin(ang)
    odd_quarter = (jnp.arange(HEAD_DIM) // (HEAD_DIM // 4)) % 2 == 1
    sa = jnp.where(odd_quarter, 0.0, -sin)
    sb = jnp.where(odd_quarter, sin, 0.0)
    return cos, sa, sb


def kernel(x, c, ctx, c_ctx, w_mod, b_mod, norm1_g, norm2_g, w_in, q_norm_g, k_norm_g, w_pool, pool_scale,
           conv_w, w_out, w_ff1, w_ff2):
    b, n, _ = x.shape
    m = ctx.shape[1]
    depth = w_mod.shape[0]
    assert b + 1 <= MOD_ROWS

    cos, sa, sb = _rope_tables(n)
    cond = jnp.concatenate([c, c_ctx[None, :], jnp.zeros((MOD_ROWS - b - 1, D_MODEL), F32)], axis=0)
    mod_all = _modulation(cond, w_mod, b_mod)

    w_in_b = w_in.astype(BF16)
    w_out_b = w_out.astype(BF16)
    w_pool_b = w_pool.astype(BF16)
    w1_b = w_ff1.astype(BF16)
    w2_b = w_ff2.astype(BF16)

    tm_lat, tm_ctx = 512, m
    xc = ctx
    for l in range(depth):
        last = l == depth - 1
        mod = mod_all[l, :b][:, None, :]
        modc = jnp.broadcast_to(mod_all[l, b][None, None, :], (b, 1, 6 * D_MODEL))
        g1 = norm1_g[l][None, :]
        g2 = norm2_g[l][None, :]
        qg = q_norm_g[l][None, :]
        kg = k_norm_g[l][None, :]
        ps = pool_scale[l][None, :]

        qc, kct, vc, puc, cbc, zc = _in_projection(xc, modc, g1, w_in_b, qg, kg, cos, sa, sb,
                                                   layer=l, tm=tm_ctx, rope=False)
        q, kt, v, pu, cb, z = _in_projection(x, mod, g1, w_in_b, qg, kg, cos, sa, sb,
                                             layer=l, tm=tm_lat, rope=True)
        kt_all = jnp.concatenate([kct, kt], axis=2)
        vx_all = _with_ones(jnp.concatenate([vc, v], axis=1))
        att = _attention(q, kt_all, vx_all, tq=512, tk=768)
        x1, h2 = _mix_out(x, att, pu, z, cb, mod, g2, w_out_b, w_pool_b, ps, conv_w[l], layer=l, tm=tm_lat)
        x = _mlp(h2, x1, mod, w1_b, w2_b, layer=l, tm=512, tf=1024)

        if not last:
            attc = _attention(qc, kct, _with_ones(vc), tq=128, tk=m)
            xc1, hc2 = _mix_out(xc, attc, puc, zc, cbc, modc, g2, w_out_b, w_pool_b, ps, conv_w[l],
                                layer=l, tm=tm_ctx)
            xc = _mlp(hc2, xc1, modc, w1_b, w2_b, layer=l, tm=tm_ctx, tf=1024)
    return x
```

```python
import functools
import math

import jax
import jax.numpy as jnp
from jax import lax
from jax.experimental import pallas as pl
from jax.experimental.pallas import tpu as pltpu

D_MODEL = 2048
GRID_W = 64
HEAD_DIM = 128
ATTN_WIDTH = D_MODEL // 2
ATTN_HEADS = ATTN_WIDTH // HEAD_DIM
KV_HEADS = ATTN_HEADS // 4
GQA_GROUP = ATTN_HEADS // KV_HEADS
KV_WIDTH = KV_HEADS * HEAD_DIM
POOL_WIDTH = D_MODEL // 4
POOL_WINDOWS = (2, 4, 8, 16)
POOL_GROUP = POOL_WIDTH // len(POOL_WINDOWS)
CONV_WIDTH = D_MODEL // 4
CONV_K = 3
D_FF = 4 * D_MODEL
ROPE_BASE = 10000.0
EPS = 1e-6
ATTN_SCALE = 1.0 / math.sqrt(HEAD_DIM)
Q_SCALE_LOG2 = ATTN_SCALE * math.log2(math.e)

K_OFF = ATTN_WIDTH
V_OFF = K_OFF + KV_WIDTH
POOL_OFF = V_OFF + KV_WIDTH
CB_OFF = POOL_OFF + POOL_WIDTH
CC_OFF = CB_OFF + CONV_WIDTH
CV_OFF = CC_OFF + CONV_WIDTH
IN_WIDTH = CV_OFF + CONV_WIDTH

HALO = 8
MOD_ROWS = 8
MIX_SUB_ROWS = 256
MIB = 1024 * 1024

F32 = jnp.float32
BF16 = jnp.bfloat16


def _resident(block_shape, index_map):
    return pl.BlockSpec(block_shape, index_map, pipeline_mode=pl.Buffered(1))


def _mod_kernel(a_ref, w_ref, b_ref, o_ref):
    a = a_ref[...]
    act = a / (1.0 + jnp.exp(-a))
    o_ref[...] = jnp.dot(act.astype(BF16), w_ref[...].astype(BF16),
                         preferred_element_type=F32) + b_ref[...]


def _modulation(cond, w_mod, b_mod):
    depth = w_mod.shape[0]
    tn = 1024
    return pl.pallas_call(
        _mod_kernel,
        grid=(depth, 6 * D_MODEL // tn),
        in_specs=[
            pl.BlockSpec((MOD_ROWS, D_MODEL), lambda l, j: (0, 0)),
            pl.BlockSpec((None, D_MODEL, tn), lambda l, j: (l, 0, j)),
            pl.BlockSpec((None, 1, tn), lambda l, j: (l, 0, j)),
        ],
        out_specs=pl.BlockSpec((None, MOD_ROWS, tn), lambda l, j: (l, 0, j)),
        out_shape=jax.ShapeDtypeStruct((depth, MOD_ROWS, 6 * D_MODEL), F32),
        compiler_params=pltpu.CompilerParams(
            dimension_semantics=("arbitrary", "arbitrary"), vmem_limit_bytes=40 * MIB),
        name="modulation",
    )(cond, w_mod, b_mod.reshape(depth, 1, 6 * D_MODEL))


def _head_rmsnorm(x, g):
    return x * lax.rsqrt(jnp.mean(x * x, axis=-1, keepdims=True) + EPS) * g


def _inproj_kernel(x_ref, mod_ref, g1_ref, w_ref, qg_ref, kg_ref, cos_ref, sa_ref, sb_ref,
                   q_ref, kt_ref, vx_ref, pu_ref, cb_ref, z_ref, *, rope):
    x = x_ref[...]
    rinv = lax.rsqrt(jnp.mean(x * x, axis=-1, keepdims=True) + EPS)
    sh = mod_ref[:, 0:D_MODEL]
    gain = g1_ref[...] * (1.0 + mod_ref[:, D_MODEL:2 * D_MODEL])
    h = (x * rinv * gain + sh).astype(BF16)

    def proj(lo, width):
        return jnp.dot(h, w_ref[:, lo:lo + width], preferred_element_type=F32)

    def rotary(t):
        if not rope:
            return t
        return (t * cos_ref[...] + pltpu.roll(t, 3 * HEAD_DIM // 4, axis=1) * sa_ref[...]
                + pltpu.roll(t, HEAD_DIM // 4, axis=1) * sb_ref[...])

    pq = proj(0, ATTN_WIDTH)
    for hh in range(ATTN_HEADS):
        t = _head_rmsnorm(pq[:, hh * HEAD_DIM:(hh + 1) * HEAD_DIM], qg_ref[...])
        q_ref[:, hh * HEAD_DIM:(hh + 1) * HEAD_DIM] = (rotary(t) * Q_SCALE_LOG2).astype(BF16)

    pk = proj(K_OFF, KV_WIDTH)
    for hh in range(KV_HEADS):
        t = rotary(_head_rmsnorm(pk[:, hh * HEAD_DIM:(hh + 1) * HEAD_DIM], kg_ref[...]))
        kt_ref[hh * HEAD_DIM:(hh + 1) * HEAD_DIM, :] = t.T.astype(BF16)

    pv = proj(V_OFF, KV_WIDTH)
    for hh in range(KV_HEADS):
        vx_ref[:, 2 * hh * HEAD_DIM:(2 * hh + 1) * HEAD_DIM] = pv[:, hh * HEAD_DIM:(hh + 1) * HEAD_DIM].astype(BF16)
        vx_ref[:, (2 * hh + 1) * HEAD_DIM:(2 * hh + 2) * HEAD_DIM] = jnp.ones((x.shape[0], HEAD_DIM), BF16)
    pu_ref[...] = proj(POOL_OFF, POOL_WIDTH)
    cb_ref[...] = proj(CB_OFF, CONV_WIDTH)
    z_ref[...] = proj(CC_OFF, CONV_WIDTH) * proj(CV_OFF, CONV_WIDTH)


def _in_projection(x, mod, g1, w_in, qg, kg, cos, sa, sb, *, layer, tm, rope):
    b, n, _ = x.shape
    row = lambda width: pl.BlockSpec((None, tm, width), lambda bi, i: (bi, i, 0))
    tab = pl.BlockSpec((tm, HEAD_DIM), lambda bi, i: (i, 0))
    vec = lambda width: pl.BlockSpec((1, width), lambda bi, i: (0, 0))
    out_shape = (
        jax.ShapeDtypeStruct((b, n, ATTN_WIDTH), BF16),
        jax.ShapeDtypeStruct((b, KV_WIDTH, n), BF16),
        jax.ShapeDtypeStruct((b, n, 2 * KV_WIDTH), BF16),
        jax.ShapeDtypeStruct((b, n, POOL_WIDTH), F32),
        jax.ShapeDtypeStruct((b, n, CONV_WIDTH), F32),
        jax.ShapeDtypeStruct((b, n, CONV_WIDTH), F32),
    )
    return pl.pallas_call(
        functools.partial(_inproj_kernel, rope=rope),
        grid=(b, n // tm),
        in_specs=[
            row(D_MODEL),
            pl.BlockSpec((None, 1, 6 * D_MODEL), lambda bi, i: (bi, 0, 0)),
            vec(D_MODEL),
            _resident((None, D_MODEL, IN_WIDTH), lambda bi, i: (layer, 0, 0)),
            vec(HEAD_DIM), vec(HEAD_DIM),
            tab, tab, tab,
        ],
        out_specs=(
            row(ATTN_WIDTH),
            pl.BlockSpec((None, KV_WIDTH, tm), lambda bi, i: (bi, 0, i)),
            row(2 * KV_WIDTH), row(POOL_WIDTH), row(CONV_WIDTH), row(CONV_WIDTH),
        ),
        out_shape=out_shape,
        compiler_params=pltpu.CompilerParams(
            dimension_semantics=("arbitrary", "arbitrary"), vmem_limit_bytes=56 * MIB),
        name="in_projection",
    )(x, mod, g1, w_in, qg, kg, cos, sa, sb)


def _attn_kernel(q_ref, kct_ref, vxc_ref, kt_ref, vx_ref, o_ref, s_ref, m_ref, acc_ref, *, tq, tk, n_chunks):
    qs = jnp.concatenate([q_ref[:, g * HEAD_DIM:(g + 1) * HEAD_DIM] for g in range(GQA_GROUP)], axis=0)

    def scores(c, slot):
        s_ref[slot] = jnp.dot(qs, kt_ref[:, pl.ds(c * tk, tk)], preferred_element_type=F32)

    def update(c, slot):
        s = s_ref[slot]
        m = m_ref[...]
        m_new = jnp.maximum(m, jnp.max(s, axis=-1, keepdims=True))
        p = jnp.exp2(s - m_new).astype(BF16)
        pv = jnp.dot(p, vx_ref[pl.ds(c * tk, tk), :], preferred_element_type=F32)
        acc_ref[...] = jnp.exp2(m - m_new) * acc_ref[...] + pv
        m_ref[...] = m_new

    sc = jnp.dot(qs, kct_ref[...], preferred_element_type=F32)
    if n_chunks:
        scores(0, 0)
    m0 = jnp.max(sc, axis=-1, keepdims=True)
    m_ref[...] = m0
    acc_ref[...] = jnp.dot(jnp.exp2(sc - m0).astype(BF16), vxc_ref[...], preferred_element_type=F32)

    def pair(i, carry):
        c = 2 * i
        scores(c + 1, 1)
        update(c, 0)
        scores(c + 2, 0)
        update(c + 1, 1)
        return carry

    if n_chunks:
        n_pairs = (n_chunks - 1) // 2
        if n_pairs:
            lax.fori_loop(0, n_pairs, pair, 0)
        if n_chunks - 1 - 2 * n_pairs:
            scores(n_chunks - 1, 1)
            update(n_chunks - 2, 0)
            update(n_chunks - 1, 1)
        else:
            update(n_chunks - 1, 0)
    acc = acc_ref[...]
    out = (acc[:, :HEAD_DIM] / acc[:, HEAD_DIM:]).astype(o_ref.dtype)
    for g in range(GQA_GROUP):
        o_ref[:, g * HEAD_DIM:(g + 1) * HEAD_DIM] = out[g * tq:(g + 1) * tq]


def _attention(q, kct, vxc, kt, vx, *, tq, tk):
    b, n, _ = q.shape
    m = vxc.shape[1]
    has_lat = kt is not None
    if not has_lat:
        kt, vx = kct, vxc
    nk = vx.shape[1]
    grp = GQA_GROUP * HEAD_DIM
    return pl.pallas_call(
        functools.partial(_attn_kernel, tq=tq, tk=tk, n_chunks=nk // tk if has_lat else 0),
        grid=(b, KV_HEADS, n // tq),
        in_specs=[
            pl.BlockSpec((None, tq, grp), lambda bi, h, i: (bi, i, h)),
            pl.BlockSpec((None, HEAD_DIM, m), lambda bi, h, i: (bi, h, 0)),
            pl.BlockSpec((None, m, 2 * HEAD_DIM), lambda bi, h, i: (bi, 0, h)),
            pl.BlockSpec((None, HEAD_DIM, nk), lambda bi, h, i: (bi, h, 0)),
            pl.BlockSpec((None, nk, 2 * HEAD_DIM), lambda bi, h, i: (bi, 0, h)),
        ],
        out_specs=pl.BlockSpec((None, tq, grp), lambda bi, h, i: (bi, i, h)),
        out_shape=jax.ShapeDtypeStruct((b, n, ATTN_WIDTH), BF16),
        scratch_shapes=[
            pltpu.VMEM((2, GQA_GROUP * tq, tk), F32),
            pltpu.VMEM((GQA_GROUP * tq, 1), F32),
            pltpu.VMEM((GQA_GROUP * tq, 2 * HEAD_DIM), F32),
        ],
        compiler_params=pltpu.CompilerParams(
            dimension_semantics=("arbitrary", "arbitrary", "arbitrary"), vmem_limit_bytes=58 * MIB),
        name="attention",
    )(q, kct, vxc, kt, vx)


def _mixout_kernel(x_ref, att_ref, pu_ref, pu_prev_ref, pu_next_ref, z_ref, z_prev_ref, z_next_ref, cb_ref,
                   mod_ref, g2_ref, wout_ref, wpool_ref, pscale_ref, convw_ref,
                   x1_ref, h2_ref, epu_ref, ez_ref, cat_ref, *, tm, n, n_sub):
    j = pl.program_id(1)
    first = j == 0
    last = j == pl.num_programs(1) - 1

    epu_ref[0:HALO, :] = jnp.where(first, 0.0, pu_prev_ref[...])
    epu_ref[HALO:HALO + tm, :] = pu_ref[...]
    epu_ref[HALO + tm:, :] = jnp.where(last, 0.0, pu_next_ref[...])
    ez_ref[0:HALO, :] = jnp.where(first, 0.0, z_prev_ref[...])
    ez_ref[HALO:HALO + tm, :] = z_ref[...]
    ez_ref[HALO + tm:, :] = jnp.where(last, 0.0, z_next_ref[...])

    cat_ref[:, 0:ATTN_WIDTH] = att_ref[...]

    rs = tm // n_sub
    for r in range(n_sub):
        r0 = r * rs
        pos = (j * tm + r0 + lax.broadcasted_iota(jnp.int32, (rs, 1), 0)).astype(F32)
        for gi, w in enumerate(POOL_WINDOWS):
            lo = w // 2
            hi = w - 1 - lo
            cols = slice(gi * POOL_GROUP, (gi + 1) * POOL_GROUP)
            tot = epu_ref[HALO + r0 - lo:HALO + r0 - lo + rs, cols]
            for o in range(-lo + 1, hi + 1):
                tot = tot + epu_ref[HALO + r0 + o:HALO + r0 + o + rs, cols]
            cnt = jnp.minimum(pos + (hi + 1), float(n)) - jnp.maximum(pos - lo, 0.0)
            d = tot / cnt - pu_ref[r0:r0 + rs, cols]
            yg = jnp.dot(d.astype(BF16), wpool_ref[gi], preferred_element_type=F32) * pscale_ref[:, cols]
            cat_ref[r0:r0 + rs, ATTN_WIDTH + gi * POOL_GROUP:ATTN_WIDTH + (gi + 1) * POOL_GROUP] = yg.astype(BF16)

        conv = (ez_ref[HALO + r0 - 1:HALO + r0 - 1 + rs, :] * convw_ref[0:1, :]
                + z_ref[r0:r0 + rs, :] * convw_ref[1:2, :]
                + ez_ref[HALO + r0 + 1:HALO + r0 + 1 + rs, :] * convw_ref[2:3, :])
        cat_ref[r0:r0 + rs, ATTN_WIDTH + POOL_WIDTH:] = (cb_ref[r0:r0 + rs, :] * conv).astype(BF16)

    gt1 = mod_ref[:, 2 * D_MODEL:3 * D_MODEL]
    sh2 = mod_ref[:, 3 * D_MODEL:4 * D_MODEL]
    gain2 = g2_ref[...] * (1.0 + mod_ref[:, 4 * D_MODEL:5 * D_MODEL])
    for r in range(n_sub):
        rows = slice(r * rs, (r + 1) * rs)
        x1 = x_ref[rows, :] + gt1 * jnp.dot(cat_ref[rows, :], wout_ref[...], preferred_element_type=F32)
        x1_ref[rows, :] = x1
        rinv = lax.rsqrt(jnp.mean(x1 * x1, axis=-1, keepdims=True) + EPS)
        h2_ref[rows, :] = (x1 * rinv * gain2 + sh2).astype(BF16)


def _mix_out(x, att, pu, z, cb, mod, g2, w_out, w_pool, pool_scale, conv_w, *, layer, tm):
    b, n, _ = x.shape
    hb = tm // HALO
    nhb = n // HALO
    row = lambda width: pl.BlockSpec((None, tm, width), lambda bi, i: (bi, i, 0))
    prev = lambda width: pl.BlockSpec((None, HALO, width), lambda bi, i: (bi, jnp.maximum(i * hb - 1, 0), 0))
    nxt = lambda width: pl.BlockSpec((None, HALO, width), lambda bi, i: (bi, jnp.minimum((i + 1) * hb, nhb - 1), 0))
    vec = lambda width: pl.BlockSpec((1, width), lambda bi, i: (0, 0))
    return pl.pallas_call(
        functools.partial(_mixout_kernel, tm=tm, n=n, n_sub=max(1, tm // MIX_SUB_ROWS)),
        grid=(b, n // tm),
        in_specs=[
            row(D_MODEL), row(ATTN_WIDTH),
            row(POOL_WIDTH), prev(POOL_WIDTH), nxt(POOL_WIDTH),
            row(CONV_WIDTH), prev(CONV_WIDTH), nxt(CONV_WIDTH),
            row(CONV_WIDTH),
            pl.BlockSpec((None, 1, 6 * D_MODEL), lambda bi, i: (bi, 0, 0)),
            vec(D_MODEL),
            _resident((None, D_MODEL, D_MODEL), lambda bi, i: (layer, 0, 0)),
            _resident((None, len(POOL_WINDOWS), POOL_GROUP, POOL_GROUP), lambda bi, i: (layer, 0, 0, 0)),
            vec(POOL_WIDTH),
            pl.BlockSpec((CONV_K, CONV_WIDTH), lambda bi, i: (0, 0)),
        ],
        out_specs=(row(D_MODEL), row(D_MODEL)),
        out_shape=(jax.ShapeDtypeStruct((b, n, D_MODEL), F32), jax.ShapeDtypeStruct((b, n, D_MODEL), BF16)),
        scratch_shapes=[
            pltpu.VMEM((tm + 2 * HALO, POOL_WIDTH), F32),
            pltpu.VMEM((tm + 2 * HALO, CONV_WIDTH), F32),
            pltpu.VMEM((tm, D_MODEL), BF16),
        ],
        compiler_params=pltpu.CompilerParams(
            dimension_semantics=("arbitrary", "arbitrary"), vmem_limit_bytes=56 * MIB),
        name="mix_out",
    )(x, att, pu, pu, pu, z, z, z, cb, mod, g2, w_out, w_pool, pool_scale, conv_w)


def _mlp_kernel(h_ref, x_ref, mod_ref, w1_ref, w2_ref, o_ref):
    f = pl.program_id(2)
    nf = pl.num_programs(2)
    @pl.when(f == 0)
    def _():
        o_ref[...] = jnp.zeros(o_ref.shape, F32)

    u = jnp.maximum(jnp.dot(h_ref[...], w1_ref[...], preferred_element_type=F32), 0.0)
    o_ref[...] += jnp.dot((u * u).astype(BF16), w2_ref[...], preferred_element_type=F32)

    @pl.when(f == nf - 1)
    def _():
        gt2 = mod_ref[:, 5 * D_MODEL:6 * D_MODEL]
        o_ref[...] = x_ref[...] + gt2 * o_ref[...]


def _mlp(h2, x1, mod, w1, w2, *, layer, tm, tf):
    b, n, _ = x1.shape
    return pl.pallas_call(
        _mlp_kernel,
        grid=(b, n // tm, D_FF // tf),
        in_specs=[
            pl.BlockSpec((None, tm, D_MODEL), lambda bi, i, f: (bi, i, 0)),
            pl.BlockSpec((None, tm, D_MODEL), lambda bi, i, f: (bi, i, 0)),
            pl.BlockSpec((None, 1, 6 * D_MODEL), lambda bi, i, f: (bi, 0, 0)),
            pl.BlockSpec((None, D_MODEL, tf), lambda bi, i, f: (layer, 0, f)),
            pl.BlockSpec((None, tf, D_MODEL), lambda bi, i, f: (layer, f, 0)),
        ],
        out_specs=pl.BlockSpec((None, tm, D_MODEL), lambda bi, i, f: (bi, i, 0)),
        out_shape=jax.ShapeDtypeStruct((b, n, D_MODEL), F32),
        compiler_params=pltpu.CompilerParams(
            dimension_semantics=("arbitrary", "arbitrary", "arbitrary"), vmem_limit_bytes=56 * MIB),
        name="mlp",
    )(h2, x1, mod, w1, w2)


def _rope_tables(n):
    rows = n // GRID_W
    row = jnp.repeat(jnp.arange(rows, dtype=F32), GRID_W)
    col = jnp.tile(jnp.arange(GRID_W, dtype=F32), rows)
    n_freq = HEAD_DIM // 4
    inv = ROPE_BASE ** (-jnp.arange(n_freq, dtype=F32) / n_freq)
    ang_r = row[:, None] * inv
    ang_c = col[:, None] * inv
    ang = jnp.concatenate([ang_r, ang_r, ang_c, ang_c], axis=-1)
    cos, sin = jnp.cos(ang), jnp.sin(ang)
    odd_quarter = (jnp.arange(HEAD_DIM) // (HEAD_DIM // 4)) % 2 == 1
    sa = jnp.where(odd_quarter, 0.0, -sin)
    sb = jnp.where(odd_quarter, sin, 0.0)
    return cos, sa, sb


def kernel(x, c, ctx, c_ctx, w_mod, b_mod, norm1_g, norm2_g, w_in, q_norm_g, k_norm_g, w_pool, pool_scale,
           conv_w, w_out, w_ff1, w_ff2):
    b, n, _ = x.shape
    m = ctx.shape[1]
    depth = w_mod.shape[0]
    assert b + 1 <= MOD_ROWS

    cos, sa, sb = _rope_tables(n)
    cond = jnp.concatenate([c, c_ctx[None, :], jnp.zeros((MOD_ROWS - b - 1, D_MODEL), F32)], axis=0)
    mod_all = _modulation(cond, w_mod, b_mod)

    w_in_b = w_in.astype(BF16)
    w_out_b = w_out.astype(BF16)
    w_pool_b = w_pool.astype(BF16)
    w1_b = w_ff1.astype(BF16)
    w2_b = w_ff2.astype(BF16)

    tm_lat, tm_ctx = 512, m
    xc = ctx
    for l in range(depth):
        last = l == depth - 1
        mod = mod_all[l, :b][:, None, :]
        modc = jnp.broadcast_to(mod_all[l, b][None, None, :], (b, 1, 6 * D_MODEL))
        g1 = norm1_g[l][None, :]
        g2 = norm2_g[l][None, :]
        qg = q_norm_g[l][None, :]
        kg = k_norm_g[l][None, :]
        ps = pool_scale[l][None, :]

        qc, kct, vxc, puc, cbc, zc = _in_projection(xc, modc, g1, w_in_b, qg, kg, cos, sa, sb,
                                                    layer=l, tm=tm_ctx, rope=False)
        q, kt, vx, pu, cb, z = _in_projection(x, mod, g1, w_in_b, qg, kg, cos, sa, sb,
                                              layer=l, tm=tm_lat, rope=True)
        att = _attention(q, kct, vxc, kt, vx, tq=512, tk=1024)
        x1, h2 = _mix_out(x, att, pu, z, cb, mod, g2, w_out_b, w_pool_b, ps, conv_w[l], layer=l, tm=tm_lat)
        x = _mlp(h2, x1, mod, w1_b, w2_b, layer=l, tm=512, tf=1024)

        if not last:
            attc = _attention(qc, kct, vxc, None, None, tq=128, tk=HEAD_DIM)
            xc1, hc2 = _mix_out(xc, attc, puc, zc, cbc, modc, g2, w_out_b, w_pool_b, ps, conv_w[l],
                                layer=l, tm=tm_ctx)
            xc = _mlp(hc2.reshape(1, b * m, D_MODEL), xc1.reshape(1, b * m, D_MODEL), modc[:1], w1_b, w2_b,
                      layer=l, tm=b * m, tf=1024).reshape(b, m, D_MODEL)
    return x
```

```python
import functools
import math

import jax
import jax.numpy as jnp
from jax import lax
from jax.experimental import pallas as pl
from jax.experimental.pallas import tpu as pltpu

D_MODEL = 2048
GRID_W = 64
HEAD_DIM = 128
ATTN_WIDTH = D_MODEL // 2
ATTN_HEADS = ATTN_WIDTH // HEAD_DIM
KV_HEADS = ATTN_HEADS // 4
GQA_GROUP = ATTN_HEADS // KV_HEADS
KV_WIDTH = KV_HEADS * HEAD_DIM
POOL_WIDTH = D_MODEL // 4
POOL_WINDOWS = (2, 4, 8, 16)
POOL_GROUP = POOL_WIDTH // len(POOL_WINDOWS)
CONV_WIDTH = D_MODEL // 4
CONV_K = 3
D_FF = 4 * D_MODEL
ROPE_BASE = 10000.0
EPS = 1e-6
ATTN_SCALE = 1.0 / math.sqrt(HEAD_DIM)
Q_SCALE_LOG2 = ATTN_SCALE * math.log2(math.e)

K_OFF = ATTN_WIDTH
V_OFF = K_OFF + KV_WIDTH
POOL_OFF = V_OFF + KV_WIDTH
CB_OFF = POOL_OFF + POOL_WIDTH
CC_OFF = CB_OFF + CONV_WIDTH
CV_OFF = CC_OFF + CONV_WIDTH
IN_WIDTH = CV_OFF + CONV_WIDTH

HALO = 8
MOD_ROWS = 8
MIX_SUB_ROWS = 256
MIB = 1024 * 1024

F32 = jnp.float32
BF16 = jnp.bfloat16


def _resident(block_shape, index_map):
    return pl.BlockSpec(block_shape, index_map, pipeline_mode=pl.Buffered(1))


def _mod_kernel(a_ref, w_ref, b_ref, o_ref):
    a = a_ref[...]
    act = a / (1.0 + jnp.exp(-a))
    o_ref[...] = jnp.dot(act.astype(BF16), w_ref[...].astype(BF16),
                         preferred_element_type=F32) + b_ref[...]


def _modulation(cond, w_mod, b_mod):
    depth = w_mod.shape[0]
    tn = 1024
    return pl.pallas_call(
        _mod_kernel,
        grid=(depth, 6 * D_MODEL // tn),
        in_specs=[
            pl.BlockSpec((MOD_ROWS, D_MODEL), lambda l, j: (0, 0)),
            pl.BlockSpec((None, D_MODEL, tn), lambda l, j: (l, 0, j)),
            pl.BlockSpec((None, 1, tn), lambda l, j: (l, 0, j)),
        ],
        out_specs=pl.BlockSpec((None, MOD_ROWS, tn), lambda l, j: (l, 0, j)),
        out_shape=jax.ShapeDtypeStruct((depth, MOD_ROWS, 6 * D_MODEL), F32),
        compiler_params=pltpu.CompilerParams(
            dimension_semantics=("arbitrary", "arbitrary"), vmem_limit_bytes=40 * MIB),
        name="modulation",
    )(cond, w_mod, b_mod.reshape(depth, 1, 6 * D_MODEL))


def _head_rmsnorm(x, g):
    return x * lax.rsqrt(jnp.mean(x * x, axis=-1, keepdims=True) + EPS) * g


def _inproj_kernel(x_ref, mod_ref, g1_ref, w_ref, qg_ref, kg_ref, cos_ref, sa_ref, sb_ref,
                   q_ref, kt_ref, vx_ref, pu_ref, cb_ref, z_ref, *, rope):
    x = x_ref[...]
    rinv = lax.rsqrt(jnp.mean(x * x, axis=-1, keepdims=True) + EPS)
    sh = mod_ref[:, 0:D_MODEL]
    gain = g1_ref[...] * (1.0 + mod_ref[:, D_MODEL:2 * D_MODEL])
    h = (x * rinv * gain + sh).astype(BF16)

    def proj(lo, width):
        return jnp.dot(h, w_ref[:, lo:lo + width], preferred_element_type=F32)

    def rotary(t):
        if not rope:
            return t
        return (t * cos_ref[...] + pltpu.roll(t, 3 * HEAD_DIM // 4, axis=1) * sa_ref[...]
                + pltpu.roll(t, HEAD_DIM // 4, axis=1) * sb_ref[...])

    pq = proj(0, ATTN_WIDTH)
    for hh in range(ATTN_HEADS):
        t = _head_rmsnorm(pq[:, hh * HEAD_DIM:(hh + 1) * HEAD_DIM], qg_ref[...])
        q_ref[:, hh * HEAD_DIM:(hh + 1) * HEAD_DIM] = (rotary(t) * Q_SCALE_LOG2).astype(BF16)

    pk = proj(K_OFF, KV_WIDTH)
    for hh in range(KV_HEADS):
        t = rotary(_head_rmsnorm(pk[:, hh * HEAD_DIM:(hh + 1) * HEAD_DIM], kg_ref[...]))
        kt_ref[hh * HEAD_DIM:(hh + 1) * HEAD_DIM, :] = t.T.astype(BF16)

    pv = proj(V_OFF, KV_WIDTH)
    for hh in range(KV_HEADS):
        vx_ref[:, 2 * hh * HEAD_DIM:(2 * hh + 1) * HEAD_DIM] = pv[:, hh * HEAD_DIM:(hh + 1) * HEAD_DIM].astype(BF16)
        vx_ref[:, (2 * hh + 1) * HEAD_DIM:(2 * hh + 2) * HEAD_DIM] = jnp.ones((x.shape[0], HEAD_DIM), BF16)
    pu_ref[...] = proj(POOL_OFF, POOL_WIDTH)
    cb_ref[...] = proj(CB_OFF, CONV_WIDTH)
    z_ref[...] = proj(CC_OFF, CONV_WIDTH) * proj(CV_OFF, CONV_WIDTH)


def _in_projection(x, mod, g1, w_in, qg, kg, cos, sa, sb, *, layer, tm, rope):
    b, n, _ = x.shape
    row = lambda width: pl.BlockSpec((None, tm, width), lambda bi, i: (bi, i, 0))
    tab = pl.BlockSpec((tm, HEAD_DIM), lambda bi, i: (i, 0))
    vec = lambda width: pl.BlockSpec((1, width), lambda bi, i: (0, 0))
    out_shape = (
        jax.ShapeDtypeStruct((b, n, ATTN_WIDTH), BF16),
        jax.ShapeDtypeStruct((b, KV_WIDTH, n), BF16),
        jax.ShapeDtypeStruct((b, n, 2 * KV_WIDTH), BF16),
        jax.ShapeDtypeStruct((b, n, POOL_WIDTH), F32),
        jax.ShapeDtypeStruct((b, n, CONV_WIDTH), F32),
        jax.ShapeDtypeStruct((b, n, CONV_WIDTH), F32),
    )
    return pl.pallas_call(
        functools.partial(_inproj_kernel, rope=rope),
        grid=(b, n // tm),
        in_specs=[
            row(D_MODEL),
            pl.BlockSpec((None, 1, 6 * D_MODEL), lambda bi, i: (bi, 0, 0)),
            vec(D_MODEL),
            _resident((None, D_MODEL, IN_WIDTH), lambda bi, i: (layer, 0, 0)),
            vec(HEAD_DIM), vec(HEAD_DIM),
            tab, tab, tab,
        ],
        out_specs=(
            row(ATTN_WIDTH),
            pl.BlockSpec((None, KV_WIDTH, tm), lambda bi, i: (bi, 0, i)),
            row(2 * KV_WIDTH), row(POOL_WIDTH), row(CONV_WIDTH), row(CONV_WIDTH),
        ),
        out_shape=out_shape,
        compiler_params=pltpu.CompilerParams(
            dimension_semantics=("arbitrary", "arbitrary"), vmem_limit_bytes=56 * MIB),
        name="in_projection",
    )(x, mod, g1, w_in, qg, kg, cos, sa, sb)


def _attn_kernel(q_ref, kct_ref, vxc_ref, kt_ref, vx_ref, o_ref, s_ref, m_ref, acc_ref, *, tq, tk, n_chunks):
    qs = jnp.concatenate([q_ref[:, g * HEAD_DIM:(g + 1) * HEAD_DIM] for g in range(GQA_GROUP)], axis=0)

    def scores(c, slot):
        s_ref[slot] = jnp.dot(qs, kt_ref[:, pl.ds(c * tk, tk)], preferred_element_type=F32)

    def update(c, slot):
        s = s_ref[slot]
        m = m_ref[...]
        m_new = jnp.maximum(m, jnp.max(s, axis=-1, keepdims=True))
        p = jnp.exp2(s - m_new).astype(BF16)
        pv = jnp.dot(p, vx_ref[pl.ds(c * tk, tk), :], preferred_element_type=F32)
        acc_ref[...] = jnp.exp2(m - m_new) * acc_ref[...] + pv
        m_ref[...] = m_new

    sc = jnp.dot(qs, kct_ref[...], preferred_element_type=F32)
    if n_chunks:
        scores(0, 0)
    m0 = jnp.max(sc, axis=-1, keepdims=True)
    m_ref[...] = m0
    acc_ref[...] = jnp.dot(jnp.exp2(sc - m0).astype(BF16), vxc_ref[...], preferred_element_type=F32)

    def pair(i, carry):
        c = 2 * i
        scores(c + 1, 1)
        update(c, 0)
        scores(c + 2, 0)
        update(c + 1, 1)
        return carry

    if n_chunks:
        n_pairs = (n_chunks - 1) // 2
        if n_pairs:
            lax.fori_loop(0, n_pairs, pair, 0)
        if n_chunks - 1 - 2 * n_pairs:
            scores(n_chunks - 1, 1)
            update(n_chunks - 2, 0)
            update(n_chunks - 1, 1)
        else:
            update(n_chunks - 1, 0)
    acc = acc_ref[...]
    out = (acc[:, :HEAD_DIM] / acc[:, HEAD_DIM:]).astype(o_ref.dtype)
    for g in range(GQA_GROUP):
        o_ref[:, g * HEAD_DIM:(g + 1) * HEAD_DIM] = out[g * tq:(g + 1) * tq]


def _attention(q, kct, vxc, kt, vx, *, tq, tk):
    b, n, _ = q.shape
    m = vxc.shape[1]
    has_lat = kt is not None
    if not has_lat:
        kt, vx = kct, vxc
    nk = vx.shape[1]
    grp = GQA_GROUP * HEAD_DIM
    return pl.pallas_call(
        functools.partial(_attn_kernel, tq=tq, tk=tk, n_chunks=nk // tk if has_lat else 0),
        grid=(b, KV_HEADS, n // tq),
        in_specs=[
            pl.BlockSpec((None, tq, grp), lambda bi, h, i: (bi, i, h)),
            pl.BlockSpec((None, HEAD_DIM, m), lambda bi, h, i: (bi, h, 0)),
            pl.BlockSpec((None, m, 2 * HEAD_DIM), lambda bi, h, i: (bi, 0, h)),
            pl.BlockSpec((None, HEAD_DIM, nk), lambda bi, h, i: (bi, h, 0)),
            pl.BlockSpec((None, nk, 2 * HEAD_DIM), lambda bi, h, i: (bi, 0, h)),
        ],
        out_specs=pl.BlockSpec((None, tq, grp), lambda bi, h, i: (bi, i, h)),
        out_shape=jax.ShapeDtypeStruct((b, n, ATTN_WIDTH), BF16),
        scratch_shapes=[
            pltpu.VMEM((2, GQA_GROUP * tq, tk), F32),
            pltpu.VMEM((GQA_GROUP * tq, 1), F32),
            pltpu.VMEM((GQA_GROUP * tq, 2 * HEAD_DIM), F32),
        ],
        compiler_params=pltpu.CompilerParams(
            dimension_semantics=("arbitrary", "arbitrary", "arbitrary"), vmem_limit_bytes=58 * MIB),
        name="attention",
    )(q, kct, vxc, kt, vx)


def _mixout_kernel(x_ref, att_ref, pu_ref, pu_prev_ref, pu_next_ref, z_ref, z_prev_ref, z_next_ref, cb_ref,
                   mod_ref, g2_ref, wout_ref, wpool_ref, pscale_ref, convw_ref,
                   x1_ref, h2_ref, epu_ref, ez_ref, cat_ref, *, tm, n, n_sub):
    j = pl.program_id(1)
    first = j == 0
    last = j == pl.num_programs(1) - 1

    epu_ref[0:HALO, :] = jnp.where(first, 0.0, pu_prev_ref[...])
    epu_ref[HALO:HALO + tm, :] = pu_ref[...]
    epu_ref[HALO + tm:, :] = jnp.where(last, 0.0, pu_next_ref[...])
    ez_ref[0:HALO, :] = jnp.where(first, 0.0, z_prev_ref[...])
    ez_ref[HALO:HALO + tm, :] = z_ref[...]
    ez_ref[HALO + tm:, :] = jnp.where(last, 0.0, z_next_ref[...])

    cat_ref[:, 0:ATTN_WIDTH] = att_ref[...]

    rs = tm // n_sub
    for r in range(n_sub):
        r0 = r * rs
        pos = (j * tm + r0 + lax.broadcasted_iota(jnp.int32, (rs, 1), 0)).astype(F32)
        for gi, w in enumerate(POOL_WINDOWS):
            lo = w // 2
            hi = w - 1 - lo
            cols = slice(gi * POOL_GROUP, (gi + 1) * POOL_GROUP)
            tot = epu_ref[HALO + r0 - lo:HALO + r0 - lo + rs, cols]
            for o in range(-lo + 1, hi + 1):
                tot = tot + epu_ref[HALO + r0 + o:HALO + r0 + o + rs, cols]
            cnt = jnp.minimum(pos + (hi + 1), float(n)) - jnp.maximum(pos - lo, 0.0)
            d = tot / cnt - pu_ref[r0:r0 + rs, cols]
            yg = jnp.dot(d.astype(BF16), wpool_ref[gi], preferred_element_type=F32) * pscale_ref[:, cols]
            cat_ref[r0:r0 + rs, ATTN_WIDTH + gi * POOL_GROUP:ATTN_WIDTH + (gi + 1) * POOL_GROUP] = yg.astype(BF16)

        conv = (ez_ref[HALO + r0 - 1:HALO + r0 - 1 + rs, :] * convw_ref[0:1, :]
                + z_ref[r0:r0 + rs, :] * convw_ref[1:2, :]
                + ez_ref[HALO + r0 + 1:HALO + r0 + 1 + rs, :] * convw_ref[2:3, :])
        cat_ref[r0:r0 + rs, ATTN_WIDTH + POOL_WIDTH:] = (cb_ref[r0:r0 + rs, :] * conv).astype(BF16)

    gt1 = mod_ref[:, 2 * D_MODEL:3 * D_MODEL]
    sh2 = mod_ref[:, 3 * D_MODEL:4 * D_MODEL]
    gain2 = g2_ref[...] * (1.0 + mod_ref[:, 4 * D_MODEL:5 * D_MODEL])
    for r in range(n_sub):
        rows = slice(r * rs, (r + 1) * rs)
        x1 = x_ref[rows, :] + gt1 * jnp.dot(cat_ref[rows, :], wout_ref[...], preferred_element_type=F32)
        x1_ref[rows, :] = x1
        rinv = lax.rsqrt(jnp.mean(x1 * x1, axis=-1, keepdims=True) + EPS)
        h2_ref[rows, :] = (x1 * rinv * gain2 + sh2).astype(BF16)


def _mix_out(x, att, pu, z, cb, mod, g2, w_out, w_pool, pool_scale, conv_w, *, layer, tm):
    b, n, _ = x.shape
    hb = tm // HALO
    nhb = n // HALO
    row = lambda width: pl.BlockSpec((None, tm, width), lambda bi, i: (bi, i, 0))
    prev = lambda width: pl.BlockSpec((None, HALO, width), lambda bi, i: (bi, jnp.maximum(i * hb - 1, 0), 0))
    nxt = lambda width: pl.BlockSpec((None, HALO, width), lambda bi, i: (bi, jnp.minimum((i + 1) * hb, nhb - 1), 0))
    vec = lambda width: pl.BlockSpec((1, width), lambda bi, i: (0, 0))
    return pl.pallas_call(
        functools.partial(_mixout_kernel, tm=tm, n=n, n_sub=max(1, tm // MIX_SUB_ROWS)),
        grid=(b, n // tm),
        in_specs=[
            row(D_MODEL), row(ATTN_WIDTH),
            row(POOL_WIDTH), prev(POOL_WIDTH), nxt(POOL_WIDTH),
            row(CONV_WIDTH), prev(CONV_WIDTH), nxt(CONV_WIDTH),
            row(CONV_WIDTH),
            pl.BlockSpec((None, 1, 6 * D_MODEL), lambda bi, i: (bi, 0, 0)),
            vec(D_MODEL),
            _resident((None, D_MODEL, D_MODEL), lambda bi, i: (layer, 0, 0)),
            _resident((None, len(POOL_WINDOWS), POOL_GROUP, POOL_GROUP), lambda bi, i: (layer, 0, 0, 0)),
            vec(POOL_WIDTH),
            pl.BlockSpec((CONV_K, CONV_WIDTH), lambda bi, i: (0, 0)),
        ],
        out_specs=(row(D_MODEL), row(D_MODEL)),
        out_shape=(jax.ShapeDtypeStruct((b, n, D_MODEL), F32), jax.ShapeDtypeStruct((b, n, D_MODEL), BF16)),
        scratch_shapes=[
            pltpu.VMEM((tm + 2 * HALO, POOL_WIDTH), F32),
            pltpu.VMEM((tm + 2 * HALO, CONV_WIDTH), F32),
            pltpu.VMEM((tm, D_MODEL), BF16),
        ],
        compiler_params=pltpu.CompilerParams(
            dimension_semantics=("arbitrary", "arbitrary"), vmem_limit_bytes=56 * MIB),
        name="mix_out",
    )(x, att, pu, pu, pu, z, z, z, cb, mod, g2, w_out, w_pool, pool_scale, conv_w)


def _mlp_kernel(h_ref, x_hbm, mod_ref, w1_ref, w2_ref, o_ref, xbuf, xsem, *, tm):
    bi = pl.program_id(0)
    i = pl.program_id(1)
    f = pl.program_id(2)
    nf = pl.num_programs(2)

    def residual_copy():
        return pltpu.make_async_copy(x_hbm.at[bi, pl.ds(i * tm, tm), :], xbuf, xsem)

    @pl.when(f == 0)
    def _():
        residual_copy().start()
        o_ref[...] = jnp.zeros(o_ref.shape, F32)

    u = jnp.maximum(jnp.dot(h_ref[...], w1_ref[...], preferred_element_type=F32), 0.0)
    o_ref[...] += jnp.dot((u * u).astype(BF16), w2_ref[...], preferred_element_type=F32)

    @pl.when(f == nf - 1)
    def _():
        residual_copy().wait()
        gt2 = mod_ref[:, 5 * D_MODEL:6 * D_MODEL]
        o_ref[...] = xbuf[...] + gt2 * o_ref[...]


def _mlp(h2, x1, mod, w1, w2, *, layer, tm, tf):
    b, n, _ = x1.shape
    return pl.pallas_call(
        functools.partial(_mlp_kernel, tm=tm),
        grid=(b, n // tm, D_FF // tf),
        in_specs=[
            pl.BlockSpec((None, tm, D_MODEL), lambda bi, i, f: (bi, i, 0)),
            pl.BlockSpec(memory_space=pl.ANY),
            pl.BlockSpec((None, 1, 6 * D_MODEL), lambda bi, i, f: (bi, 0, 0)),
            pl.BlockSpec((None, D_MODEL, tf), lambda bi, i, f: (layer, 0, f)),
            pl.BlockSpec((None, tf, D_MODEL), lambda bi, i, f: (layer, f, 0)),
        ],
        out_specs=pl.BlockSpec((None, tm, D_MODEL), lambda bi, i, f: (bi, i, 0)),
        out_shape=jax.ShapeDtypeStruct((b, n, D_MODEL), F32),
        scratch_shapes=[pltpu.VMEM((tm, D_MODEL), F32), pltpu.SemaphoreType.DMA(())],
        compiler_params=pltpu.CompilerParams(
            dimension_semantics=("arbitrary", "arbitrary", "arbitrary"), vmem_limit_bytes=58 * MIB),
        name="mlp",
    )(h2, x1, mod, w1, w2)


def _rope_tables(n):
    rows = n // GRID_W
    row = jnp.repeat(jnp.arange(rows, dtype=F32), GRID_W)
    col = jnp.tile(jnp.arange(GRID_W, dtype=F32), rows)
    n_freq = HEAD_DIM // 4
    inv = ROPE_BASE ** (-jnp.arange(n_freq, dtype=F32) / n_freq)
    ang_r = row[:, None] * inv
    ang_c = col[:, None] * inv
    ang = jnp.concatenate([ang_r, ang_r, ang_c, ang_c], axis=-1)
    cos, sin = jnp.cos(ang), jnp.sin(ang)
    odd_quarter = (jnp.arange(HEAD_DIM) // (HEAD_DIM // 4)) % 2 == 1
    sa = jnp.where(odd_quarter, 0.0, -sin)
    sb = jnp.where(odd_quarter, sin, 0.0)
    return cos, sa, sb


def kernel(x, c, ctx, c_ctx, w_mod, b_mod, norm1_g, norm2_g, w_in, q_norm_g, k_norm_g, w_pool, pool_scale,
           conv_w, w_out, w_ff1, w_ff2):
    b, n, _ = x.shape
    m = ctx.shape[1]
    depth = w_mod.shape[0]
    assert b + 1 <= MOD_ROWS

    cos, sa, sb = _rope_tables(n)
    cond = jnp.concatenate([c, c_ctx[None, :], jnp.zeros((MOD_ROWS - b - 1, D_MODEL), F32)], axis=0)
    mod_all = _modulation(cond, w_mod, b_mod)

    w_in_b = w_in.astype(BF16)
    w_out_b = w_out.astype(BF16)
    w_pool_b = w_pool.astype(BF16)
    w1_b = w_ff1.astype(BF16)
    w2_b = w_ff2.astype(BF16)

    tm_lat, tm_ctx = 512, m
    xc = ctx
    for l in range(depth):
        last = l == depth - 1
        mod = mod_all[l, :b][:, None, :]
        modc = jnp.broadcast_to(mod_all[l, b][None, None, :], (b, 1, 6 * D_MODEL))
        g1 = norm1_g[l][None, :]
        g2 = norm2_g[l][None, :]
        qg = q_norm_g[l][None, :]
        kg = k_norm_g[l][None, :]
        ps = pool_scale[l][None, :]

        qc, kct, vxc, puc, cbc, zc = _in_projection(xc, modc, g1, w_in_b, qg, kg, cos, sa, sb,
                                                    layer=l, tm=tm_ctx, rope=False)
        q, kt, vx, pu, cb, z = _in_projection(x, mod, g1, w_in_b, qg, kg, cos, sa, sb,
                                              layer=l, tm=tm_lat, rope=True)
        att = _attention(q, kct, vxc, kt, vx, tq=512, tk=1024)
        x1, h2 = _mix_out(x, att, pu, z, cb, mod, g2, w_out_b, w_pool_b, ps, conv_w[l], layer=l, tm=tm_lat)
        x = _mlp(h2, x1, mod, w1_b, w2_b, layer=l, tm=1024, tf=1024)

        if not last:
            attc = _attention(qc, kct, vxc, None, None, tq=128, tk=HEAD_DIM)
            xc1, hc2 = _mix_out(xc, attc, puc, zc, cbc, modc, g2, w_out_b, w_pool_b, ps, conv_w[l],
                                layer=l, tm=tm_ctx)
            xc = _mlp(hc2.reshape(1, b * m, D_MODEL), xc1.reshape(1, b * m, D_MODEL), modc[:1], w1_b, w2_b,
                      layer=l, tm=b * m, tf=1024).reshape(b, m, D_MODEL)
    return x
```

```python
import functools
import math

import jax
import jax.numpy as jnp
from jax import lax
from jax.experimental import pallas as pl
from jax.experimental.pallas import tpu as pltpu

D_MODEL = 2048
GRID_W = 64
HEAD_DIM = 128
ATTN_WIDTH = D_MODEL // 2
ATTN_HEADS = ATTN_WIDTH // HEAD_DIM
KV_HEADS = ATTN_HEADS // 4
GQA_GROUP = ATTN_HEADS // KV_HEADS
KV_WIDTH = KV_HEADS * HEAD_DIM
POOL_WIDTH = D_MODEL // 4
POOL_WINDOWS = (2, 4, 8, 16)
POOL_GROUP = POOL_WIDTH // len(POOL_WINDOWS)
CONV_WIDTH = D_MODEL // 4
CONV_K = 3
D_FF = 4 * D_MODEL
ROPE_BASE = 10000.0
EPS = 1e-6
ATTN_SCALE = 1.0 / math.sqrt(HEAD_DIM)
Q_SCALE_LOG2 = ATTN_SCALE * math.log2(math.e)

K_OFF = ATTN_WIDTH
V_OFF = K_OFF + KV_WIDTH
POOL_OFF = V_OFF + KV_WIDTH
CB_OFF = POOL_OFF + POOL_WIDTH
CC_OFF = CB_OFF + CONV_WIDTH
CV_OFF = CC_OFF + CONV_WIDTH
IN_WIDTH = CV_OFF + CONV_WIDTH

HALO = 8
MOD_ROWS = 8
MIX_SUB_ROWS = 256
MIB = 1024 * 1024

F32 = jnp.float32
BF16 = jnp.bfloat16


def _resident(block_shape, index_map):
    return pl.BlockSpec(block_shape, index_map, pipeline_mode=pl.Buffered(1))


def _mod_kernel(a_ref, w_ref, b_ref, o_ref):
    a = a_ref[...]
    act = a / (1.0 + jnp.exp(-a))
    o_ref[...] = jnp.dot(act.astype(BF16), w_ref[...].astype(BF16),
                         preferred_element_type=F32) + b_ref[...]


def _modulation(cond, w_mod, b_mod):
    depth = w_mod.shape[0]
    tn = 1024
    return pl.pallas_call(
        _mod_kernel,
        grid=(depth, 6 * D_MODEL // tn),
        in_specs=[
            pl.BlockSpec((MOD_ROWS, D_MODEL), lambda l, j: (0, 0)),
            pl.BlockSpec((None, D_MODEL, tn), lambda l, j: (l, 0, j)),
            pl.BlockSpec((None, 1, tn), lambda l, j: (l, 0, j)),
        ],
        out_specs=pl.BlockSpec((None, MOD_ROWS, tn), lambda l, j: (l, 0, j)),
        out_shape=jax.ShapeDtypeStruct((depth, MOD_ROWS, 6 * D_MODEL), F32),
        compiler_params=pltpu.CompilerParams(
            dimension_semantics=("arbitrary", "arbitrary"), vmem_limit_bytes=40 * MIB),
        name="modulation",
    )(cond, w_mod, b_mod.reshape(depth, 1, 6 * D_MODEL))


def _head_rmsnorm(x, g):
    return x * lax.rsqrt(jnp.mean(x * x, axis=-1, keepdims=True) + EPS) * g


def _inproj_kernel(x_ref, mod_ref, g1_ref, w_ref, qg_ref, kg_ref, cos_ref, sa_ref, sb_ref,
                   q_ref, kt_ref, vx_ref, pu_ref, cb_ref, z_ref, *, rope):
    x = x_ref[...]
    rinv = lax.rsqrt(jnp.mean(x * x, axis=-1, keepdims=True) + EPS)
    sh = mod_ref[:, 0:D_MODEL]
    gain = g1_ref[...] * (1.0 + mod_ref[:, D_MODEL:2 * D_MODEL])
    h = (x * rinv * gain + sh).astype(BF16)

    def proj(lo, width):
        return jnp.dot(h, w_ref[:, lo:lo + width], preferred_element_type=F32)

    def rotary(t):
        if not rope:
            return t
        return (t * cos_ref[...] + pltpu.roll(t, 3 * HEAD_DIM // 4, axis=1) * sa_ref[...]
                + pltpu.roll(t, HEAD_DIM // 4, axis=1) * sb_ref[...])

    pq = proj(0, ATTN_WIDTH)
    for hh in range(ATTN_HEADS):
        t = _head_rmsnorm(pq[:, hh * HEAD_DIM:(hh + 1) * HEAD_DIM], qg_ref[...])
        q_ref[:, hh * HEAD_DIM:(hh + 1) * HEAD_DIM] = (rotary(t) * Q_SCALE_LOG2).astype(BF16)

    pk = proj(K_OFF, KV_WIDTH)
    for hh in range(KV_HEADS):
        t = rotary(_head_rmsnorm(pk[:, hh * HEAD_DIM:(hh + 1) * HEAD_DIM], kg_ref[...]))
        kt_ref[hh * HEAD_DIM:(hh + 1) * HEAD_DIM, :] = t.T.astype(BF16)

    pv = proj(V_OFF, KV_WIDTH)
    for hh in range(KV_HEADS):
        vx_ref[:, 2 * hh * HEAD_DIM:(2 * hh + 1) * HEAD_DIM] = pv[:, hh * HEAD_DIM:(hh + 1) * HEAD_DIM].astype(BF16)
        vx_ref[:, (2 * hh + 1) * HEAD_DIM:(2 * hh + 2) * HEAD_DIM] = jnp.ones((x.shape[0], HEAD_DIM), BF16)
    pu_ref[...] = proj(POOL_OFF, POOL_WIDTH)
    cb_ref[...] = proj(CB_OFF, CONV_WIDTH)
    z_ref[...] = proj(CC_OFF, CONV_WIDTH) * proj(CV_OFF, CONV_WIDTH)


def _in_projection(x, mod, g1, w_in, qg, kg, cos, sa, sb, *, layer, tm, rope):
    b, n, _ = x.shape
    row = lambda width: pl.BlockSpec((None, tm, width), lambda bi, i: (bi, i, 0))
    tab = pl.BlockSpec((tm, HEAD_DIM), lambda bi, i: (i, 0))
    vec = lambda width: pl.BlockSpec((1, width), lambda bi, i: (0, 0))
    out_shape = (
        jax.ShapeDtypeStruct((b, n, ATTN_WIDTH), BF16),
        jax.ShapeDtypeStruct((b, KV_WIDTH, n), BF16),
        jax.ShapeDtypeStruct((b, n, 2 * KV_WIDTH), BF16),
        jax.ShapeDtypeStruct((b, n, POOL_WIDTH), F32),
        jax.ShapeDtypeStruct((b, n, CONV_WIDTH), F32),
        jax.ShapeDtypeStruct((b, n, CONV_WIDTH), F32),
    )
    return pl.pallas_call(
        functools.partial(_inproj_kernel, rope=rope),
        grid=(b, n // tm),
        in_specs=[
            row(D_MODEL),
            pl.BlockSpec((None, 1, 6 * D_MODEL), lambda bi, i: (bi, 0, 0)),
            vec(D_MODEL),
            _resident((None, D_MODEL, IN_WIDTH), lambda bi, i: (layer, 0, 0)),
            vec(HEAD_DIM), vec(HEAD_DIM),
            tab, tab, tab,
        ],
        out_specs=(
            row(ATTN_WIDTH),
            pl.BlockSpec((None, KV_WIDTH, tm), lambda bi, i: (bi, 0, i)),
            row(2 * KV_WIDTH), row(POOL_WIDTH), row(CONV_WIDTH), row(CONV_WIDTH),
        ),
        out_shape=out_shape,
        compiler_params=pltpu.CompilerParams(
            dimension_semantics=("arbitrary", "arbitrary"), vmem_limit_bytes=56 * MIB),
        name="in_projection",
    )(x, mod, g1, w_in, qg, kg, cos, sa, sb)


def _attn_kernel(*refs, tq, tk, n_chunks, n_cast):
    q_ref, kct_ref, vxc_ref, kt_ref, vx_ref = refs[:5]
    cast_src = refs[5:5 + n_cast]
    o_ref = refs[5 + n_cast]
    cast_dst = refs[6 + n_cast:6 + 2 * n_cast]
    s_ref, m_ref, acc_ref = refs[6 + 2 * n_cast:]
    for src, dst in zip(cast_src, cast_dst):
        dst[...] = src[...].astype(BF16)

    qs = jnp.concatenate([q_ref[:, g * HEAD_DIM:(g + 1) * HEAD_DIM] for g in range(GQA_GROUP)], axis=0)

    def scores(c, slot):
        s_ref[slot] = jnp.dot(qs, kt_ref[:, pl.ds(c * tk, tk)], preferred_element_type=F32)

    def update(c, slot):
        s = s_ref[slot]
        m = m_ref[...]
        m_new = jnp.maximum(m, jnp.max(s, axis=-1, keepdims=True))
        p = jnp.exp2(s - m_new).astype(BF16)
        pv = jnp.dot(p, vx_ref[pl.ds(c * tk, tk), :], preferred_element_type=F32)
        acc_ref[...] = jnp.exp2(m - m_new) * acc_ref[...] + pv
        m_ref[...] = m_new

    sc = jnp.dot(qs, kct_ref[...], preferred_element_type=F32)
    if n_chunks:
        scores(0, 0)
    m0 = jnp.max(sc, axis=-1, keepdims=True)
    m_ref[...] = m0
    acc_ref[...] = jnp.dot(jnp.exp2(sc - m0).astype(BF16), vxc_ref[...], preferred_element_type=F32)

    def pair(i, carry):
        c = 2 * i
        scores(c + 1, 1)
        update(c, 0)
        scores(c + 2, 0)
        update(c + 1, 1)
        return carry

    if n_chunks:
        n_pairs = (n_chunks - 1) // 2
        if n_pairs:
            lax.fori_loop(0, n_pairs, pair, 0)
        if n_chunks - 1 - 2 * n_pairs:
            scores(n_chunks - 1, 1)
            update(n_chunks - 2, 0)
            update(n_chunks - 1, 1)
        else:
            update(n_chunks - 1, 0)
    acc = acc_ref[...]
    out = (acc[:, :HEAD_DIM] / acc[:, HEAD_DIM:]).astype(o_ref.dtype)
    for g in range(GQA_GROUP):
        o_ref[:, g * HEAD_DIM:(g + 1) * HEAD_DIM] = out[g * tq:(g + 1) * tq]


def _attention(q, kct, vxc, kt, vx, *, tq, tk, cast=(), layer=0):
    b, n, _ = q.shape
    m = vxc.shape[1]
    has_lat = kt is not None
    if not has_lat:
        kt, vx = kct, vxc
    nk = vx.shape[1]
    grp = GQA_GROUP * HEAD_DIM
    nq = n // tq
    n_steps = b * KV_HEADS * nq
    step = lambda bi, h, i: (bi * KV_HEADS + h) * nq + i
    cast_in_specs = [pl.BlockSpec((None, w.shape[1] // n_steps, w.shape[2]),
                                  lambda bi, h, i: (layer, step(bi, h, i), 0)) for w in cast]
    cast_out_specs = [pl.BlockSpec((None, w.shape[1] // n_steps, w.shape[2]),
                                   lambda bi, h, i: (0, step(bi, h, i), 0)) for w in cast]
    cast_out_shapes = [jax.ShapeDtypeStruct((1,) + w.shape[1:], BF16) for w in cast]
    res = _resident if has_lat else pl.BlockSpec
    outs = pl.pallas_call(
        functools.partial(_attn_kernel, tq=tq, tk=tk, n_chunks=nk // tk if has_lat else 0, n_cast=len(cast)),
        grid=(b, KV_HEADS, nq),
        in_specs=[
            pl.BlockSpec((None, tq, grp), lambda bi, h, i: (bi, i, h)),
            pl.BlockSpec((None, HEAD_DIM, m), lambda bi, h, i: (bi, h, 0)),
            pl.BlockSpec((None, m, 2 * HEAD_DIM), lambda bi, h, i: (bi, 0, h)),
            res((None, HEAD_DIM, nk), lambda bi, h, i: (bi, h, 0)),
            res((None, nk, 2 * HEAD_DIM), lambda bi, h, i: (bi, 0, h)),
        ] + cast_in_specs,
        out_specs=[pl.BlockSpec((None, tq, grp), lambda bi, h, i: (bi, i, h))] + cast_out_specs,
        out_shape=[jax.ShapeDtypeStruct((b, n, ATTN_WIDTH), BF16)] + cast_out_shapes,
        scratch_shapes=[
            pltpu.VMEM((2, GQA_GROUP * tq, tk), F32),
            pltpu.VMEM((GQA_GROUP * tq, 1), F32),
            pltpu.VMEM((GQA_GROUP * tq, 2 * HEAD_DIM), F32),
        ],
        compiler_params=pltpu.CompilerParams(
            dimension_semantics=("arbitrary", "arbitrary", "arbitrary"), vmem_limit_bytes=58 * MIB),
        name="attention",
    )(q, kct, vxc, kt, vx, *cast)
    return outs[0], tuple(outs[1:])


def _mixout_kernel(x_ref, att_ref, pu_ref, pu_prev_ref, pu_next_ref, z_ref, z_prev_ref, z_next_ref, cb_ref,
                   mod_ref, g2_ref, wout_ref, wpool_ref, pscale_ref, convw_ref,
                   x1_ref, h2_ref, epu_ref, ez_ref, cat_ref, *, tm, n, n_sub):
    j = pl.program_id(1)
    first = j == 0
    last = j == pl.num_programs(1) - 1

    epu_ref[0:HALO, :] = jnp.where(first, 0.0, pu_prev_ref[...])
    epu_ref[HALO:HALO + tm, :] = pu_ref[...]
    epu_ref[HALO + tm:, :] = jnp.where(last, 0.0, pu_next_ref[...])
    ez_ref[0:HALO, :] = jnp.where(first, 0.0, z_prev_ref[...])
    ez_ref[HALO:HALO + tm, :] = z_ref[...]
    ez_ref[HALO + tm:, :] = jnp.where(last, 0.0, z_next_ref[...])

    cat_ref[:, 0:ATTN_WIDTH] = att_ref[...]

    rs = tm // n_sub
    for r in range(n_sub):
        r0 = r * rs
        pos = (j * tm + r0 + lax.broadcasted_iota(jnp.int32, (rs, 1), 0)).astype(F32)
        for gi, w in enumerate(POOL_WINDOWS):
            lo = w // 2
            hi = w - 1 - lo
            cols = slice(gi * POOL_GROUP, (gi + 1) * POOL_GROUP)
            tot = epu_ref[HALO + r0 - lo:HALO + r0 - lo + rs, cols]
            for o in range(-lo + 1, hi + 1):
                tot = tot + epu_ref[HALO + r0 + o:HALO + r0 + o + rs, cols]
            cnt = jnp.minimum(pos + (hi + 1), float(n)) - jnp.maximum(pos - lo, 0.0)
            d = tot / cnt - pu_ref[r0:r0 + rs, cols]
            yg = jnp.dot(d.astype(BF16), wpool_ref[gi], preferred_element_type=F32) * pscale_ref[:, cols]
            cat_ref[r0:r0 + rs, ATTN_WIDTH + gi * POOL_GROUP:ATTN_WIDTH + (gi + 1) * POOL_GROUP] = yg.astype(BF16)

        conv = (ez_ref[HALO + r0 - 1:HALO + r0 - 1 + rs, :] * convw_ref[0:1, :]
                + z_ref[r0:r0 + rs, :] * convw_ref[1:2, :]
                + ez_ref[HALO + r0 + 1:HALO + r0 + 1 + rs, :] * convw_ref[2:3, :])
        cat_ref[r0:r0 + rs, ATTN_WIDTH + POOL_WIDTH:] = (cb_ref[r0:r0 + rs, :] * conv).astype(BF16)

    gt1 = mod_ref[:, 2 * D_MODEL:3 * D_MODEL]
    sh2 = mod_ref[:, 3 * D_MODEL:4 * D_MODEL]
    gain2 = g2_ref[...] * (1.0 + mod_ref[:, 4 * D_MODEL:5 * D_MODEL])
    for r in range(n_sub):
        rows = slice(r * rs, (r + 1) * rs)
        x1 = x_ref[rows, :] + gt1 * jnp.dot(cat_ref[rows, :], wout_ref[...], preferred_element_type=F32)
        x1_ref[rows, :] = x1
        rinv = lax.rsqrt(jnp.mean(x1 * x1, axis=-1, keepdims=True) + EPS)
        h2_ref[rows, :] = (x1 * rinv * gain2 + sh2).astype(BF16)


def _mix_out(x, att, pu, z, cb, mod, g2, w_out, w_pool, pool_scale, conv_w, *, layer, tm):
    b, n, _ = x.shape
    hb = tm // HALO
    nhb = n // HALO
    row = lambda width: pl.BlockSpec((None, tm, width), lambda bi, i: (bi, i, 0))
    prev = lambda width: pl.BlockSpec((None, HALO, width), lambda bi, i: (bi, jnp.maximum(i * hb - 1, 0), 0))
    nxt = lambda width: pl.BlockSpec((None, HALO, width), lambda bi, i: (bi, jnp.minimum((i + 1) * hb, nhb - 1), 0))
    vec = lambda width: pl.BlockSpec((1, width), lambda bi, i: (0, 0))
    return pl.pallas_call(
        functools.partial(_mixout_kernel, tm=tm, n=n, n_sub=max(1, tm // MIX_SUB_ROWS)),
        grid=(b, n // tm),
        in_specs=[
            row(D_MODEL), row(ATTN_WIDTH),
            row(POOL_WIDTH), prev(POOL_WIDTH), nxt(POOL_WIDTH),
            row(CONV_WIDTH), prev(CONV_WIDTH), nxt(CONV_WIDTH),
            row(CONV_WIDTH),
            pl.BlockSpec((None, 1, 6 * D_MODEL), lambda bi, i: (bi, 0, 0)),
            vec(D_MODEL),
            _resident((None, D_MODEL, D_MODEL), lambda bi, i: (layer, 0, 0)),
            _resident((None, len(POOL_WINDOWS), POOL_GROUP, POOL_GROUP), lambda bi, i: (layer, 0, 0, 0)),
            vec(POOL_WIDTH),
            pl.BlockSpec((CONV_K, CONV_WIDTH), lambda bi, i: (0, 0)),
        ],
        out_specs=(row(D_MODEL), row(D_MODEL)),
        out_shape=(jax.ShapeDtypeStruct((b, n, D_MODEL), F32), jax.ShapeDtypeStruct((b, n, D_MODEL), BF16)),
        scratch_shapes=[
            pltpu.VMEM((tm + 2 * HALO, POOL_WIDTH), F32),
            pltpu.VMEM((tm + 2 * HALO, CONV_WIDTH), F32),
            pltpu.VMEM((tm, D_MODEL), BF16),
        ],
        compiler_params=pltpu.CompilerParams(
            dimension_semantics=("arbitrary", "arbitrary"), vmem_limit_bytes=56 * MIB),
        name="mix_out",
    )(x, att, pu, pu, pu, z, z, z, cb, mod, g2, w_out, w_pool, pool_scale, conv_w)


def _mlp_kernel(h_ref, x_hbm, mod_ref, w1_ref, w2_ref, o_ref, xbuf, xsem, *, tm):
    bi = pl.program_id(0)
    i = pl.program_id(1)
    f = pl.program_id(2)
    nf = pl.num_programs(2)

    def residual_copy():
        return pltpu.make_async_copy(x_hbm.at[bi, pl.ds(i * tm, tm), :], xbuf, xsem)

    @pl.when(f == 0)
    def _():
        residual_copy().start()
        o_ref[...] = jnp.zeros(o_ref.shape, F32)

    u = jnp.maximum(jnp.dot(h_ref[...], w1_ref[...], preferred_element_type=F32), 0.0)
    o_ref[...] += jnp.dot((u * u).astype(BF16), w2_ref[...], preferred_element_type=F32)

    @pl.when(f == nf - 1)
    def _():
        residual_copy().wait()
        gt2 = mod_ref[:, 5 * D_MODEL:6 * D_MODEL]
        o_ref[...] = xbuf[...] + gt2 * o_ref[...]


def _mlp(h2, x1, mod, w1, w2, *, layer, tm, tf):
    b, n, _ = x1.shape
    return pl.pallas_call(
        functools.partial(_mlp_kernel, tm=tm),
        grid=(b, n // tm, D_FF // tf),
        in_specs=[
            pl.BlockSpec((None, tm, D_MODEL), lambda bi, i, f: (bi, i, 0)),
            pl.BlockSpec(memory_space=pl.ANY),
            pl.BlockSpec((None, 1, 6 * D_MODEL), lambda bi, i, f: (bi, 0, 0)),
            pl.BlockSpec((None, D_MODEL, tf), lambda bi, i, f: (layer, 0, f)),
            pl.BlockSpec((None, tf, D_MODEL), lambda bi, i, f: (layer, f, 0)),
        ],
        out_specs=pl.BlockSpec((None, tm, D_MODEL), lambda bi, i, f: (bi, i, 0)),
        out_shape=jax.ShapeDtypeStruct((b, n, D_MODEL), F32),
        scratch_shapes=[pltpu.VMEM((tm, D_MODEL), F32), pltpu.SemaphoreType.DMA(())],
        compiler_params=pltpu.CompilerParams(
            dimension_semantics=("arbitrary", "arbitrary", "arbitrary"), vmem_limit_bytes=58 * MIB),
        name="mlp",
    )(h2, x1, mod, w1, w2)


def _rope_tables(n):
    rows = n // GRID_W
    row = jnp.repeat(jnp.arange(rows, dtype=F32), GRID_W)
    col = jnp.tile(jnp.arange(GRID_W, dtype=F32), rows)
    n_freq = HEAD_DIM // 4
    inv = ROPE_BASE ** (-jnp.arange(n_freq, dtype=F32) / n_freq)
    ang_r = row[:, None] * inv
    ang_c = col[:, None] * inv
    ang = jnp.concatenate([ang_r, ang_r, ang_c, ang_c], axis=-1)
    cos, sin = jnp.cos(ang), jnp.sin(ang)
    odd_quarter = (jnp.arange(HEAD_DIM) // (HEAD_DIM // 4)) % 2 == 1
    sa = jnp.where(odd_quarter, 0.0, -sin)
    sb = jnp.where(odd_quarter, sin, 0.0)
    return cos, sa, sb


def kernel(x, c, ctx, c_ctx, w_mod, b_mod, norm1_g, norm2_g, w_in, q_norm_g, k_norm_g, w_pool, pool_scale,
           conv_w, w_out, w_ff1, w_ff2):
    b, n, _ = x.shape
    m = ctx.shape[1]
    depth = w_mod.shape[0]
    assert b + 1 <= MOD_ROWS

    cos, sa, sb = _rope_tables(n)
    cond = jnp.concatenate([c, c_ctx[None, :], jnp.zeros((MOD_ROWS - b - 1, D_MODEL), F32)], axis=0)
    mod_all = _modulation(cond, w_mod, b_mod)

    w_in_b = w_in.astype(BF16)
    w_pool_b = w_pool.astype(BF16)

    tm_lat, tm_ctx = 512, m
    xc = ctx
    for l in range(depth):
        last = l == depth - 1
        mod = mod_all[l, :b][:, None, :]
        modc = jnp.broadcast_to(mod_all[l, b][None, None, :], (b, 1, 6 * D_MODEL))
        g1 = norm1_g[l][None, :]
        g2 = norm2_g[l][None, :]
        qg = q_norm_g[l][None, :]
        kg = k_norm_g[l][None, :]
        ps = pool_scale[l][None, :]

        qc, kct, vxc, puc, cbc, zc = _in_projection(xc, modc, g1, w_in_b, qg, kg, cos, sa, sb,
                                                    layer=l, tm=tm_ctx, rope=False)
        q, kt, vx, pu, cb, z = _in_projection(x, mod, g1, w_in_b, qg, kg, cos, sa, sb,
                                              layer=l, tm=tm_lat, rope=True)
        att, (w_out_b, w1_b, w2_b) = _attention(q, kct, vxc, kt, vx, tq=512, tk=1024,
                                                cast=(w_out, w_ff1, w_ff2), layer=l)
        wp = w_pool_b[l][None]
        x1, h2 = _mix_out(x, att, pu, z, cb, mod, g2, w_out_b, wp, ps, conv_w[l], layer=0, tm=tm_lat)
        x = _mlp(h2, x1, mod, w1_b, w2_b, layer=0, tm=1024, tf=1024)

        if not last:
            attc, _ = _attention(qc, kct, vxc, None, None, tq=128, tk=HEAD_DIM)
            xc1, hc2 = _mix_out(xc, attc, puc, zc, cbc, modc, g2, w_out_b, wp, ps, conv_w[l],
                                layer=0, tm=tm_ctx)
            xc = _mlp(hc2.reshape(1, b * m, D_MODEL), xc1.reshape(1, b * m, D_MODEL), modc[:1], w1_b, w2_b,
                      layer=0, tm=b * m, tf=1024).reshape(b, m, D_MODEL)
    return x
```

```python
import functools
import math

import jax
import jax.numpy as jnp
from jax import lax
from jax.experimental import pallas as pl
from jax.experimental.pallas import tpu as pltpu

D_MODEL = 2048
GRID_W = 64
HEAD_DIM = 128
ATTN_WIDTH = D_MODEL // 2
ATTN_HEADS = ATTN_WIDTH // HEAD_DIM
KV_HEADS = ATTN_HEADS // 4
GQA_GROUP = ATTN_HEADS // KV_HEADS
KV_WIDTH = KV_HEADS * HEAD_DIM
POOL_WIDTH = D_MODEL // 4
POOL_WINDOWS = (2, 4, 8, 16)
POOL_GROUP = POOL_WIDTH // len(POOL_WINDOWS)
CONV_WIDTH = D_MODEL // 4
CONV_K = 3
D_FF = 4 * D_MODEL
ROPE_BASE = 10000.0
EPS = 1e-6
ATTN_SCALE = 1.0 / math.sqrt(HEAD_DIM)
Q_SCALE_LOG2 = ATTN_SCALE * math.log2(math.e)

K_OFF = ATTN_WIDTH
V_OFF = K_OFF + KV_WIDTH
POOL_OFF = V_OFF + KV_WIDTH
CB_OFF = POOL_OFF + POOL_WIDTH
CC_OFF = CB_OFF + CONV_WIDTH
CV_OFF = CC_OFF + CONV_WIDTH
IN_WIDTH = CV_OFF + CONV_WIDTH

HALO = 8
MOD_ROWS = 8
MIB = 1024 * 1024

MOD_TN = 1024
ROW_TILE = 512
MIX_SUB_ROWS = 256
ATTN_TQ = 512
ATTN_TK = 1024
MLP_TM = 1024
MLP_TF = 1024
VMEM_SMALL = 40 * MIB
VMEM_MEDIUM = 56 * MIB
VMEM_LARGE = 58 * MIB

F32 = jnp.float32
BF16 = jnp.bfloat16


def _resident(block_shape, index_map):
    return pl.BlockSpec(block_shape, index_map, pipeline_mode=pl.Buffered(1))


def _mod_kernel(a_ref, w_ref, b_ref, o_ref):
    a = a_ref[...]
    act = a / (1.0 + jnp.exp(-a))
    o_ref[...] = jnp.dot(act.astype(BF16), w_ref[...].astype(BF16),
                         preferred_element_type=F32) + b_ref[...]


def _modulation(cond, w_mod, b_mod):
    depth = w_mod.shape[0]
    return pl.pallas_call(
        _mod_kernel,
        grid=(depth, 6 * D_MODEL // MOD_TN),
        in_specs=[
            pl.BlockSpec((MOD_ROWS, D_MODEL), lambda l, j: (0, 0)),
            pl.BlockSpec((None, D_MODEL, MOD_TN), lambda l, j: (l, 0, j)),
            pl.BlockSpec((None, 1, MOD_TN), lambda l, j: (l, 0, j)),
        ],
        out_specs=pl.BlockSpec((None, MOD_ROWS, MOD_TN), lambda l, j: (l, 0, j)),
        out_shape=jax.ShapeDtypeStruct((depth, MOD_ROWS, 6 * D_MODEL), F32),
        compiler_params=pltpu.CompilerParams(
            dimension_semantics=("arbitrary", "arbitrary"), vmem_limit_bytes=VMEM_SMALL),
        name="modulation",
    )(cond, w_mod, b_mod.reshape(depth, 1, 6 * D_MODEL))


def _head_rmsnorm(x, g):
    return x * lax.rsqrt(jnp.mean(x * x, axis=-1, keepdims=True) + EPS) * g


def _inproj_kernel(x_ref, mod_ref, g1_ref, w_ref, qg_ref, kg_ref, cos_ref, sa_ref, sb_ref,
                   q_ref, kt_ref, vx_ref, pu_ref, cb_ref, z_ref, *, rope):
    x = x_ref[...]
    rinv = lax.rsqrt(jnp.mean(x * x, axis=-1, keepdims=True) + EPS)
    sh = mod_ref[:, 0:D_MODEL]
    gain = g1_ref[...] * (1.0 + mod_ref[:, D_MODEL:2 * D_MODEL])
    h = (x * rinv * gain + sh).astype(BF16)

    def proj(lo, width):
        return jnp.dot(h, w_ref[:, lo:lo + width], preferred_element_type=F32)

    def rotary(t):
        if not rope:
            return t
        return (t * cos_ref[...] + pltpu.roll(t, 3 * HEAD_DIM // 4, axis=1) * sa_ref[...]
                + pltpu.roll(t, HEAD_DIM // 4, axis=1) * sb_ref[...])

    pq = proj(0, ATTN_WIDTH)
    for hh in range(ATTN_HEADS):
        t = _head_rmsnorm(pq[:, hh * HEAD_DIM:(hh + 1) * HEAD_DIM], qg_ref[...])
        q_ref[:, hh * HEAD_DIM:(hh + 1) * HEAD_DIM] = (rotary(t) * Q_SCALE_LOG2).astype(BF16)

    pk = proj(K_OFF, KV_WIDTH)
    for hh in range(KV_HEADS):
        t = rotary(_head_rmsnorm(pk[:, hh * HEAD_DIM:(hh + 1) * HEAD_DIM], kg_ref[...]))
        kt_ref[hh * HEAD_DIM:(hh + 1) * HEAD_DIM, :] = t.T.astype(BF16)

    pv = proj(V_OFF, KV_WIDTH)
    for hh in range(KV_HEADS):
        vx_ref[:, 2 * hh * HEAD_DIM:(2 * hh + 1) * HEAD_DIM] = pv[:, hh * HEAD_DIM:(hh + 1) * HEAD_DIM].astype(BF16)
        vx_ref[:, (2 * hh + 1) * HEAD_DIM:(2 * hh + 2) * HEAD_DIM] = jnp.ones((x.shape[0], HEAD_DIM), BF16)
    pu_ref[...] = proj(POOL_OFF, POOL_WIDTH)
    cb_ref[...] = proj(CB_OFF, CONV_WIDTH)
    z_ref[...] = proj(CC_OFF, CONV_WIDTH) * proj(CV_OFF, CONV_WIDTH)


def _in_projection(x, mod, g1, w_in, qg, kg, cos, sa, sb, *, tm, rope):
    b, n, _ = x.shape
    row = lambda width: pl.BlockSpec((None, tm, width), lambda bi, i: (bi, i, 0))
    tab = pl.BlockSpec((tm, HEAD_DIM), lambda bi, i: (i, 0))
    vec = lambda width: pl.BlockSpec((1, width), lambda bi, i: (0, 0))
    out_shape = (
        jax.ShapeDtypeStruct((b, n, ATTN_WIDTH), BF16),
        jax.ShapeDtypeStruct((b, KV_WIDTH, n), BF16),
        jax.ShapeDtypeStruct((b, n, 2 * KV_WIDTH), BF16),
        jax.ShapeDtypeStruct((b, n, POOL_WIDTH), F32),
        jax.ShapeDtypeStruct((b, n, CONV_WIDTH), F32),
        jax.ShapeDtypeStruct((b, n, CONV_WIDTH), F32),
    )
    return pl.pallas_call(
        functools.partial(_inproj_kernel, rope=rope),
        grid=(b, n // tm),
        in_specs=[
            row(D_MODEL),
            pl.BlockSpec((None, 1, 6 * D_MODEL), lambda bi, i: (bi, 0, 0)),
            vec(D_MODEL),
            _resident((None, D_MODEL, IN_WIDTH), lambda bi, i: (0, 0, 0)),
            vec(HEAD_DIM), vec(HEAD_DIM),
            tab, tab, tab,
        ],
        out_specs=(
            row(ATTN_WIDTH),
            pl.BlockSpec((None, KV_WIDTH, tm), lambda bi, i: (bi, 0, i)),
            row(2 * KV_WIDTH), row(POOL_WIDTH), row(CONV_WIDTH), row(CONV_WIDTH),
        ),
        out_shape=out_shape,
        compiler_params=pltpu.CompilerParams(
            dimension_semantics=("arbitrary", "arbitrary"), vmem_limit_bytes=VMEM_MEDIUM),
        name="in_projection",
    )(x, mod, g1, w_in, qg, kg, cos, sa, sb)


def _attn_kernel(*refs, tq, tk, n_chunks, n_cast):
    q_ref, kct_ref, vxc_ref, kt_ref, vx_ref = refs[:5]
    cast_src = refs[5:5 + n_cast]
    o_ref = refs[5 + n_cast]
    cast_dst = refs[6 + n_cast:6 + 2 * n_cast]
    s_ref, m_ref, acc_ref = refs[6 + 2 * n_cast:]
    for src, dst in zip(cast_src, cast_dst):
        dst[...] = src[...].astype(BF16)

    qs = jnp.concatenate([q_ref[:, g * HEAD_DIM:(g + 1) * HEAD_DIM] for g in range(GQA_GROUP)], axis=0)

    def scores(c, slot):
        s_ref[slot] = jnp.dot(qs, kt_ref[:, pl.ds(c * tk, tk)], preferred_element_type=F32)

    def update(c, slot):
        s = s_ref[slot]
        m = m_ref[...]
        m_new = jnp.maximum(m, jnp.max(s, axis=-1, keepdims=True))
        p = jnp.exp2(s - m_new).astype(BF16)
        pv = jnp.dot(p, vx_ref[pl.ds(c * tk, tk), :], preferred_element_type=F32)
        acc_ref[...] = jnp.exp2(m - m_new) * acc_ref[...] + pv
        m_ref[...] = m_new

    sc = jnp.dot(qs, kct_ref[...], preferred_element_type=F32)
    if n_chunks:
        scores(0, 0)
    m0 = jnp.max(sc, axis=-1, keepdims=True)
    m_ref[...] = m0
    acc_ref[...] = jnp.dot(jnp.exp2(sc - m0).astype(BF16), vxc_ref[...], preferred_element_type=F32)

    def pair(i, carry):
        c = 2 * i
        scores(c + 1, 1)
        update(c, 0)
        scores(c + 2, 0)
        update(c + 1, 1)
        return carry

    if n_chunks:
        n_pairs = (n_chunks - 1) // 2
        if n_pairs:
            lax.fori_loop(0, n_pairs, pair, 0)
        if n_chunks - 1 - 2 * n_pairs:
            scores(n_chunks - 1, 1)
            update(n_chunks - 2, 0)
            update(n_chunks - 1, 1)
        else:
            update(n_chunks - 1, 0)
    acc = acc_ref[...]
    out = (acc[:, :HEAD_DIM] / acc[:, HEAD_DIM:]).astype(o_ref.dtype)
    for g in range(GQA_GROUP):
        o_ref[:, g * HEAD_DIM:(g + 1) * HEAD_DIM] = out[g * tq:(g + 1) * tq]


def _attention(q, kct, vxc, kt, vx, *, tq, tk, cast=()):
    b, n, _ = q.shape
    m = vxc.shape[1]
    has_lat = kt is not None
    if not has_lat:
        kt, vx = kct, vxc
    nk = vx.shape[1]
    grp = GQA_GROUP * HEAD_DIM
    nq = n // tq
    n_steps = b * KV_HEADS * nq
    step = lambda bi, h, i: (bi * KV_HEADS + h) * nq + i

    def slab_spec(w, layer):
        return pl.BlockSpec((None, w.shape[1] // n_steps, w.shape[2]), lambda bi, h, i: (layer, step(bi, h, i), 0))

    cast_in_specs = [slab_spec(w, layer) for w, layer in cast]
    cast_out_specs = [slab_spec(w, 0) for w, _ in cast]
    cast_out_shapes = [jax.ShapeDtypeStruct((1,) + w.shape[1:], BF16) for w, _ in cast]
    res = _resident if has_lat else pl.BlockSpec
    outs = pl.pallas_call(
        functools.partial(_attn_kernel, tq=tq, tk=tk, n_chunks=nk // tk if has_lat else 0, n_cast=len(cast)),
        grid=(b, KV_HEADS, nq),
        in_specs=[
            pl.BlockSpec((None, tq, grp), lambda bi, h, i: (bi, i, h)),
            pl.BlockSpec((None, HEAD_DIM, m), lambda bi, h, i: (bi, h, 0)),
            pl.BlockSpec((None, m, 2 * HEAD_DIM), lambda bi, h, i: (bi, 0, h)),
            res((None, HEAD_DIM, nk), lambda bi, h, i: (bi, h, 0)),
            res((None, nk, 2 * HEAD_DIM), lambda bi, h, i: (bi, 0, h)),
        ] + cast_in_specs,
        out_specs=[pl.BlockSpec((None, tq, grp), lambda bi, h, i: (bi, i, h))] + cast_out_specs,
        out_shape=[jax.ShapeDtypeStruct((b, n, ATTN_WIDTH), BF16)] + cast_out_shapes,
        scratch_shapes=[
            pltpu.VMEM((2, GQA_GROUP * tq, tk), F32),
            pltpu.VMEM((GQA_GROUP * tq, 1), F32),
            pltpu.VMEM((GQA_GROUP * tq, 2 * HEAD_DIM), F32),
        ],
        compiler_params=pltpu.CompilerParams(
            dimension_semantics=("arbitrary", "arbitrary", "arbitrary"), vmem_limit_bytes=VMEM_LARGE),
        name="attention",
    )(q, kct, vxc, kt, vx, *[w for w, _ in cast])
    return outs[0], tuple(outs[1:])


def _mixout_kernel(x_ref, att_ref, pu_ref, pu_prev_ref, pu_next_ref, z_ref, z_prev_ref, z_next_ref, cb_ref,
                   mod_ref, g2_ref, wout_ref, wpool_ref, pscale_ref, convw_ref,
                   x1_ref, h2_ref, epu_ref, ez_ref, cat_ref, *, tm, n, n_sub):
    j = pl.program_id(1)
    first = j == 0
    last = j == pl.num_programs(1) - 1

    epu_ref[0:HALO, :] = jnp.where(first, 0.0, pu_prev_ref[...])
    epu_ref[HALO:HALO + tm, :] = pu_ref[...]
    epu_ref[HALO + tm:, :] = jnp.where(last, 0.0, pu_next_ref[...])
    ez_ref[0:HALO, :] = jnp.where(first, 0.0, z_prev_ref[...])
    ez_ref[HALO:HALO + tm, :] = z_ref[...]
    ez_ref[HALO + tm:, :] = jnp.where(last, 0.0, z_next_ref[...])

    cat_ref[:, 0:ATTN_WIDTH] = att_ref[...]

    rs = tm // n_sub
    for r in range(n_sub):
        r0 = r * rs
        pos = (j * tm + r0 + lax.broadcasted_iota(jnp.int32, (rs, 1), 0)).astype(F32)
        for gi, w in enumerate(POOL_WINDOWS):
            lo = w // 2
            hi = w - 1 - lo
            cols = slice(gi * POOL_GROUP, (gi + 1) * POOL_GROUP)
            tot = epu_ref[HALO + r0 - lo:HALO + r0 - lo + rs, cols]
            for o in range(-lo + 1, hi + 1):
                tot = tot + epu_ref[HALO + r0 + o:HALO + r0 + o + rs, cols]
            cnt = jnp.minimum(pos + (hi + 1), float(n)) - jnp.maximum(pos - lo, 0.0)
            d = tot / cnt - pu_ref[r0:r0 + rs, cols]
            yg = jnp.dot(d.astype(BF16), wpool_ref[gi], preferred_element_type=F32) * pscale_ref[:, cols]
            cat_ref[r0:r0 + rs, ATTN_WIDTH + gi * POOL_GROUP:ATTN_WIDTH + (gi + 1) * POOL_GROUP] = yg.astype(BF16)

        conv = (ez_ref[HALO + r0 - 1:HALO + r0 - 1 + rs, :] * convw_ref[0:1, :]
                + z_ref[r0:r0 + rs, :] * convw_ref[1:2, :]
                + ez_ref[HALO + r0 + 1:HALO + r0 + 1 + rs, :] * convw_ref[2:3, :])
        cat_ref[r0:r0 + rs, ATTN_WIDTH + POOL_WIDTH:] = (cb_ref[r0:r0 + rs, :] * conv).astype(BF16)

    gt1 = mod_ref[:, 2 * D_MODEL:3 * D_MODEL]
    sh2 = mod_ref[:, 3 * D_MODEL:4 * D_MODEL]
    gain2 = g2_ref[...] * (1.0 + mod_ref[:, 4 * D_MODEL:5 * D_MODEL])
    for r in range(n_sub):
        rows = slice(r * rs, (r + 1) * rs)
        x1 = x_ref[rows, :] + gt1 * jnp.dot(cat_ref[rows, :], wout_ref[...], preferred_element_type=F32)
        x1_ref[rows, :] = x1
        rinv = lax.rsqrt(jnp.mean(x1 * x1, axis=-1, keepdims=True) + EPS)
        h2_ref[rows, :] = (x1 * rinv * gain2 + sh2).astype(BF16)


def _mix_out(x, att, pu, z, cb, mod, g2, w_out, w_pool, pool_scale, conv_w, *, tm):
    b, n, _ = x.shape
    hb = tm // HALO
    nhb = n // HALO
    row = lambda width: pl.BlockSpec((None, tm, width), lambda bi, i: (bi, i, 0))
    prev = lambda width: pl.BlockSpec((None, HALO, width), lambda bi, i: (bi, jnp.maximum(i * hb - 1, 0), 0))
    nxt = lambda width: pl.BlockSpec((None, HALO, width), lambda bi, i: (bi, jnp.minimum((i + 1) * hb, nhb - 1), 0))
    vec = lambda width: pl.BlockSpec((1, width), lambda bi, i: (0, 0))
    return pl.pallas_call(
        functools.partial(_mixout_kernel, tm=tm, n=n, n_sub=max(1, tm // MIX_SUB_ROWS)),
        grid=(b, n // tm),
        in_specs=[
            row(D_MODEL), row(ATTN_WIDTH),
            row(POOL_WIDTH), prev(POOL_WIDTH), nxt(POOL_WIDTH),
            row(CONV_WIDTH), prev(CONV_WIDTH), nxt(CONV_WIDTH),
            row(CONV_WIDTH),
            pl.BlockSpec((None, 1, 6 * D_MODEL), lambda bi, i: (bi, 0, 0)),
            vec(D_MODEL),
            _resident((None, D_MODEL, D_MODEL), lambda bi, i: (0, 0, 0)),
            _resident((None, len(POOL_WINDOWS), POOL_GROUP, POOL_GROUP), lambda bi, i: (0, 0, 0, 0)),
            vec(POOL_WIDTH),
            pl.BlockSpec((CONV_K, CONV_WIDTH), lambda bi, i: (0, 0)),
        ],
        out_specs=(row(D_MODEL), row(D_MODEL)),
        out_shape=(jax.ShapeDtypeStruct((b, n, D_MODEL), F32), jax.ShapeDtypeStruct((b, n, D_MODEL), BF16)),
        scratch_shapes=[
            pltpu.VMEM((tm + 2 * HALO, POOL_WIDTH), F32),
            pltpu.VMEM((tm + 2 * HALO, CONV_WIDTH), F32),
            pltpu.VMEM((tm, D_MODEL), BF16),
        ],
        compiler_params=pltpu.CompilerParams(
            dimension_semantics=("arbitrary", "arbitrary"), vmem_limit_bytes=VMEM_MEDIUM),
        name="mix_out",
    )(x, att, pu, pu, pu, z, z, z, cb, mod, g2, w_out, w_pool, pool_scale, conv_w)


def _mlp_kernel(h_ref, x_hbm, mod_ref, w1_ref, w2_ref, o_ref, xbuf, xsem, *, tm):
    bi = pl.program_id(0)
    i = pl.program_id(1)
    f = pl.program_id(2)
    nf = pl.num_programs(2)

    def residual_copy():
        return pltpu.make_async_copy(x_hbm.at[bi, pl.ds(i * tm, tm), :], xbuf, xsem)

    @pl.when(f == 0)
    def _():
        residual_copy().start()
        o_ref[...] = jnp.zeros(o_ref.shape, F32)

    u = jnp.maximum(jnp.dot(h_ref[...], w1_ref[...], preferred_element_type=F32), 0.0)
    o_ref[...] += jnp.dot((u * u).astype(BF16), w2_ref[...], preferred_element_type=F32)

    @pl.when(f == nf - 1)
    def _():
        residual_copy().wait()
        gt2 = mod_ref[:, 5 * D_MODEL:6 * D_MODEL]
        o_ref[...] = xbuf[...] + gt2 * o_ref[...]


def _mlp(h2, x1, mod, w1, w2, *, tm, tf):
    b, n, _ = x1.shape
    return pl.pallas_call(
        functools.partial(_mlp_kernel, tm=tm),
        grid=(b, n // tm, D_FF // tf),
        in_specs=[
            pl.BlockSpec((None, tm, D_MODEL), lambda bi, i, f: (bi, i, 0)),
            pl.BlockSpec(memory_space=pl.ANY),
            pl.BlockSpec((None, 1, 6 * D_MODEL), lambda bi, i, f: (bi, 0, 0)),
            pl.BlockSpec((None, D_MODEL, tf), lambda bi, i, f: (0, 0, f)),
            pl.BlockSpec((None, tf, D_MODEL), lambda bi, i, f: (0, f, 0)),
        ],
        out_specs=pl.BlockSpec((None, tm, D_MODEL), lambda bi, i, f: (bi, i, 0)),
        out_shape=jax.ShapeDtypeStruct((b, n, D_MODEL), F32),
        scratch_shapes=[pltpu.VMEM((tm, D_MODEL), F32), pltpu.SemaphoreType.DMA(())],
        compiler_params=pltpu.CompilerParams(
            dimension_semantics=("arbitrary", "arbitrary", "arbitrary"), vmem_limit_bytes=VMEM_LARGE),
        name="mlp",
    )(h2, x1, mod, w1, w2)


def _rope_tables(n):
    rows = n // GRID_W
    n_freq = HEAD_DIM // 4
    inv = ROPE_BASE ** (-jnp.arange(n_freq, dtype=F32) / n_freq)
    ang_r = jnp.arange(rows, dtype=F32)[:, None] * inv
    ang_c = jnp.arange(GRID_W, dtype=F32)[:, None] * inv

    def table(fn):
        tr = jnp.repeat(fn(ang_r), GRID_W, axis=0)
        tc = jnp.tile(fn(ang_c), (rows, 1))
        return jnp.concatenate([tr, tr, tc, tc], axis=-1)

    cos, sin = table(jnp.cos), table(jnp.sin)
    odd_quarter = (jnp.arange(HEAD_DIM) // (HEAD_DIM // 4)) % 2 == 1
    sa = jnp.where(odd_quarter, 0.0, -sin)
    sb = jnp.where(odd_quarter, sin, 0.0)
    return cos, sa, sb


def kernel(x, c, ctx, c_ctx, w_mod, b_mod, norm1_g, norm2_g, w_in, q_norm_g, k_norm_g, w_pool, pool_scale,
           conv_w, w_out, w_ff1, w_ff2):
    b, n, _ = x.shape
    m = ctx.shape[1]
    depth = w_mod.shape[0]
    assert b + 1 <= MOD_ROWS

    cos, sa, sb = _rope_tables(n)
    cond = jnp.concatenate([c, c_ctx[None, :], jnp.zeros((MOD_ROWS - b - 1, D_MODEL), F32)], axis=0)
    mod_all = _modulation(cond, w_mod, b_mod)

    w_in_b = w_in[0:1].astype(BF16)
    w_pool_b = w_pool.astype(BF16)

    tm_ctx = m
    xc = ctx
    for l in range(depth):
        last = l == depth - 1
        mod = mod_all[l, :b][:, None, :]
        modc = jnp.broadcast_to(mod_all[l, b][None, None, :], (b, 1, 6 * D_MODEL))
        g1 = norm1_g[l][None, :]
        g2 = norm2_g[l][None, :]
        qg = q_norm_g[l][None, :]
        kg = k_norm_g[l][None, :]
        ps = pool_scale[l][None, :]
        wp = w_pool_b[l][None]

        qc, kct, vxc, puc, cbc, zc = _in_projection(xc, modc, g1, w_in_b, qg, kg, cos, sa, sb,
                                                    tm=tm_ctx, rope=False)
        q, kt, vx, pu, cb, z = _in_projection(x, mod, g1, w_in_b, qg, kg, cos, sa, sb, tm=ROW_TILE, rope=True)
        cast = [(w_out, l), (w_ff1, l), (w_ff2, l)] + ([] if last else [(w_in, l + 1)])
        att, w_b = _attention(q, kct, vxc, kt, vx, tq=ATTN_TQ, tk=ATTN_TK, cast=cast)
        w_out_b, w1_b, w2_b = w_b[:3]
        x1, h2 = _mix_out(x, att, pu, z, cb, mod, g2, w_out_b, wp, ps, conv_w[l], tm=ROW_TILE)
        x = _mlp(h2, x1, mod, w1_b, w2_b, tm=MLP_TM, tf=MLP_TF)

        if not last:
            attc, _ = _attention(qc, kct, vxc, None, None, tq=HEAD_DIM, tk=HEAD_DIM)
            xc1, hc2 = _mix_out(xc, attc, puc, zc, cbc, modc, g2, w_out_b, wp, ps, conv_w[l], tm=tm_ctx)
            xc = _mlp(hc2.reshape(1, b * m, D_MODEL), xc1.reshape(1, b * m, D_MODEL), modc[:1], w1_b, w2_b,
                      tm=b * m, tf=MLP_TF).reshape(b, m, D_MODEL)
            w_in_b = w_b[3]
    return x
```

```python
import functools
import math

import jax
import jax.numpy as jnp
from jax import lax
from jax.experimental import pallas as pl
from jax.experimental.pallas import tpu as pltpu

D_MODEL = 2048
GRID_W = 64
HEAD_DIM = 128
ATTN_WIDTH = D_MODEL // 2
ATTN_HEADS = ATTN_WIDTH // HEAD_DIM
KV_HEADS = ATTN_HEADS // 4
GQA_GROUP = ATTN_HEADS // KV_HEADS
KV_WIDTH = KV_HEADS * HEAD_DIM
POOL_WIDTH = D_MODEL // 4
POOL_WINDOWS = (2, 4, 8, 16)
POOL_GROUP = POOL_WIDTH // len(POOL_WINDOWS)
CONV_WIDTH = D_MODEL // 4
CONV_K = 3
D_FF = 4 * D_MODEL
ROPE_BASE = 10000.0
EPS = 1e-6
ATTN_SCALE = 1.0 / math.sqrt(HEAD_DIM)
Q_SCALE_LOG2 = ATTN_SCALE * math.log2(math.e)

K_OFF = ATTN_WIDTH
V_OFF = K_OFF + KV_WIDTH
POOL_OFF = V_OFF + KV_WIDTH
CB_OFF = POOL_OFF + POOL_WIDTH
CC_OFF = CB_OFF + CONV_WIDTH
CV_OFF = CC_OFF + CONV_WIDTH
IN_WIDTH = CV_OFF + CONV_WIDTH

HALO = 8
MOD_ROWS = 8
MIB = 1024 * 1024

MOD_TN = 1024
ROW_TILE = 512
MIX_SUB_ROWS = 256
ATTN_TQ = 512
ATTN_TK = 1024
BOUNDED_UNROLL = 4
SCORE_BOUND_LOG2 = 60.0
BF16_ROUNDING_SLACK = 1.02
MLP_TM = 1024
MLP_TF = 1024
VMEM_SMALL = 40 * MIB
VMEM_MEDIUM = 56 * MIB
VMEM_LARGE = 58 * MIB

F32 = jnp.float32
BF16 = jnp.bfloat16


def _resident(block_shape, index_map):
    return pl.BlockSpec(block_shape, index_map, pipeline_mode=pl.Buffered(1))


def _mod_kernel(a_ref, w_ref, b_ref, o_ref):
    a = a_ref[...]
    act = a / (1.0 + jnp.exp(-a))
    o_ref[...] = jnp.dot(act.astype(BF16), w_ref[...].astype(BF16),
                         preferred_element_type=F32) + b_ref[...]


def _modulation(cond, w_mod, b_mod):
    depth = w_mod.shape[0]
    return pl.pallas_call(
        _mod_kernel,
        grid=(depth, 6 * D_MODEL // MOD_TN),
        in_specs=[
            pl.BlockSpec((MOD_ROWS, D_MODEL), lambda l, j: (0, 0)),
            pl.BlockSpec((None, D_MODEL, MOD_TN), lambda l, j: (l, 0, j)),
            pl.BlockSpec((None, 1, MOD_TN), lambda l, j: (l, 0, j)),
        ],
        out_specs=pl.BlockSpec((None, MOD_ROWS, MOD_TN), lambda l, j: (l, 0, j)),
        out_shape=jax.ShapeDtypeStruct((depth, MOD_ROWS, 6 * D_MODEL), F32),
        compiler_params=pltpu.CompilerParams(
            dimension_semantics=("arbitrary", "arbitrary"), vmem_limit_bytes=VMEM_SMALL),
        name="modulation",
    )(cond, w_mod, b_mod.reshape(depth, 1, 6 * D_MODEL))


def _head_rmsnorm(x, g):
    return x * lax.rsqrt(jnp.mean(x * x, axis=-1, keepdims=True) + EPS) * g


def _inproj_kernel(x_ref, mod_ref, g1_ref, w_ref, qg_ref, kg_ref, cos_ref, sa_ref, sb_ref,
                   q_ref, kt_ref, vx_ref, pu_ref, cb_ref, z_ref, *, rope):
    x = x_ref[...]
    rinv = lax.rsqrt(jnp.mean(x * x, axis=-1, keepdims=True) + EPS)
    sh = mod_ref[:, 0:D_MODEL]
    gain = g1_ref[...] * (1.0 + mod_ref[:, D_MODEL:2 * D_MODEL])
    h = (x * rinv * gain + sh).astype(BF16)

    def proj(lo, width):
        return jnp.dot(h, w_ref[:, lo:lo + width], preferred_element_type=F32)

    def rotary(t):
        if not rope:
            return t
        return (t * cos_ref[...] + pltpu.roll(t, 3 * HEAD_DIM // 4, axis=1) * sa_ref[...]
                + pltpu.roll(t, HEAD_DIM // 4, axis=1) * sb_ref[...])

    pq = proj(0, ATTN_WIDTH)
    for hh in range(ATTN_HEADS):
        t = _head_rmsnorm(pq[:, hh * HEAD_DIM:(hh + 1) * HEAD_DIM], qg_ref[...])
        q_ref[:, hh * HEAD_DIM:(hh + 1) * HEAD_DIM] = (rotary(t) * Q_SCALE_LOG2).astype(BF16)

    pk = proj(K_OFF, KV_WIDTH)
    for hh in range(KV_HEADS):
        t = rotary(_head_rmsnorm(pk[:, hh * HEAD_DIM:(hh + 1) * HEAD_DIM], kg_ref[...]))
        kt_ref[hh * HEAD_DIM:(hh + 1) * HEAD_DIM, :] = t.T.astype(BF16)

    pv = proj(V_OFF, KV_WIDTH)
    for hh in range(KV_HEADS):
        vx_ref[:, 2 * hh * HEAD_DIM:(2 * hh + 1) * HEAD_DIM] = pv[:, hh * HEAD_DIM:(hh + 1) * HEAD_DIM].astype(BF16)
        vx_ref[:, (2 * hh + 1) * HEAD_DIM:(2 * hh + 2) * HEAD_DIM] = jnp.ones((x.shape[0], HEAD_DIM), BF16)
    pu_ref[...] = proj(POOL_OFF, POOL_WIDTH)
    cb_ref[...] = proj(CB_OFF, CONV_WIDTH)
    z_ref[...] = proj(CC_OFF, CONV_WIDTH) * proj(CV_OFF, CONV_WIDTH)


def _in_projection(x, mod, g1, w_in, qg, kg, cos, sa, sb, *, tm, rope):
    b, n, _ = x.shape
    row = lambda width: pl.BlockSpec((None, tm, width), lambda bi, i: (bi, i, 0))
    tab = pl.BlockSpec((tm, HEAD_DIM), lambda bi, i: (i, 0))
    vec = lambda width: pl.BlockSpec((1, width), lambda bi, i: (0, 0))
    out_shape = (
        jax.ShapeDtypeStruct((b, n, ATTN_WIDTH), BF16),
        jax.ShapeDtypeStruct((b, KV_WIDTH, n), BF16),
        jax.ShapeDtypeStruct((b, n, 2 * KV_WIDTH), BF16),
        jax.ShapeDtypeStruct((b, n, POOL_WIDTH), F32),
        jax.ShapeDtypeStruct((b, n, CONV_WIDTH), F32),
        jax.ShapeDtypeStruct((b, n, CONV_WIDTH), F32),
    )
    return pl.pallas_call(
        functools.partial(_inproj_kernel, rope=rope),
        grid=(b, n // tm),
        in_specs=[
            row(D_MODEL),
            pl.BlockSpec((None, 1, 6 * D_MODEL), lambda bi, i: (bi, 0, 0)),
            vec(D_MODEL),
            _resident((None, D_MODEL, IN_WIDTH), lambda bi, i: (0, 0, 0)),
            vec(HEAD_DIM), vec(HEAD_DIM),
            tab, tab, tab,
        ],
        out_specs=(
            row(ATTN_WIDTH),
            pl.BlockSpec((None, KV_WIDTH, tm), lambda bi, i: (bi, 0, i)),
            row(2 * KV_WIDTH), row(POOL_WIDTH), row(CONV_WIDTH), row(CONV_WIDTH),
        ),
        out_shape=out_shape,
        compiler_params=pltpu.CompilerParams(
            dimension_semantics=("arbitrary", "arbitrary"), vmem_limit_bytes=VMEM_MEDIUM),
        name="in_projection",
    )(x, mod, g1, w_in, qg, kg, cos, sa, sb)


def _attn_kernel(*refs, tq, tk, n_chunks, n_cast):
    bounded_ref, q_ref, kct_ref, vxc_ref, kt_ref, vx_ref = refs[:6]
    cast_src = refs[6:6 + n_cast]
    o_ref = refs[6 + n_cast]
    cast_dst = refs[7 + n_cast:7 + 2 * n_cast]
    s_ref, m_ref, acc_ref = refs[7 + 2 * n_cast:]
    for src, dst in zip(cast_src, cast_dst):
        dst[...] = src[...].astype(BF16)

    qs = jnp.concatenate([q_ref[:, g * HEAD_DIM:(g + 1) * HEAD_DIM] for g in range(GQA_GROUP)], axis=0)

    def write_out():
        acc = acc_ref[...]
        out = (acc[:, :HEAD_DIM] / acc[:, HEAD_DIM:]).astype(o_ref.dtype)
        for g in range(GQA_GROUP):
            o_ref[:, g * HEAD_DIM:(g + 1) * HEAD_DIM] = out[g * tq:(g + 1) * tq]

    @pl.when(bounded_ref[0] != 0)
    def _():
        def weighted_values(kt, vx):
            p = jnp.exp2(jnp.dot(qs, kt, preferred_element_type=F32)).astype(BF16)
            return jnp.dot(p, vx, preferred_element_type=F32)

        acc_ref[...] = weighted_values(kct_ref[...], vxc_ref[...])

        def trip(i, carry):
            for u in range(BOUNDED_UNROLL):
                c = i * BOUNDED_UNROLL + u
                acc_ref[...] += weighted_values(kt_ref[:, pl.ds(c * tk, tk)], vx_ref[pl.ds(c * tk, tk), :])
            return carry

        if n_chunks:
            lax.fori_loop(0, n_chunks // BOUNDED_UNROLL, trip, 0)
        write_out()

    @pl.when(bounded_ref[0] == 0)
    def _():
        _online_softmax(qs, kct_ref, vxc_ref, kt_ref, vx_ref, s_ref, m_ref, acc_ref, tk=tk, n_chunks=n_chunks)
        write_out()


def _online_softmax(qs, kct_ref, vxc_ref, kt_ref, vx_ref, s_ref, m_ref, acc_ref, *, tk, n_chunks):
    def scores(c, slot):
        s_ref[slot] = jnp.dot(qs, kt_ref[:, pl.ds(c * tk, tk)], preferred_element_type=F32)

    def update(c, slot):
        s = s_ref[slot]
        m = m_ref[...]
        m_new = jnp.maximum(m, jnp.max(s, axis=-1, keepdims=True))
        p = jnp.exp2(s - m_new).astype(BF16)
        pv = jnp.dot(p, vx_ref[pl.ds(c * tk, tk), :], preferred_element_type=F32)
        acc_ref[...] = jnp.exp2(m - m_new) * acc_ref[...] + pv
        m_ref[...] = m_new

    sc = jnp.dot(qs, kct_ref[...], preferred_element_type=F32)
    if n_chunks:
        scores(0, 0)
    m0 = jnp.max(sc, axis=-1, keepdims=True)
    m_ref[...] = m0
    acc_ref[...] = jnp.dot(jnp.exp2(sc - m0).astype(BF16), vxc_ref[...], preferred_element_type=F32)

    def pair(i, carry):
        c = 2 * i
        scores(c + 1, 1)
        update(c, 0)
        scores(c + 2, 0)
        update(c + 1, 1)
        return carry

    if n_chunks:
        n_pairs = (n_chunks - 1) // 2
        if n_pairs:
            lax.fori_loop(0, n_pairs, pair, 0)
        if n_chunks - 1 - 2 * n_pairs:
            scores(n_chunks - 1, 1)
            update(n_chunks - 2, 0)
            update(n_chunks - 1, 1)
        else:
            update(n_chunks - 1, 0)


def _attention(q, kct, vxc, kt, vx, bounded, *, tq, tk, cast=()):
    b, n, _ = q.shape
    m = vxc.shape[1]
    has_lat = kt is not None
    if not has_lat:
        kt, vx = kct, vxc
    nk = vx.shape[1]
    grp = GQA_GROUP * HEAD_DIM
    nq = n // tq
    n_steps = b * KV_HEADS * nq
    step = lambda bi, h, i: (bi * KV_HEADS + h) * nq + i

    def slab_spec(w, layer):
        return pl.BlockSpec((None, w.shape[1] // n_steps, w.shape[2]), lambda bi, h, i: (layer, step(bi, h, i), 0))

    cast_in_specs = [slab_spec(w, layer) for w, layer in cast]
    cast_out_specs = [slab_spec(w, 0) for w, _ in cast]
    cast_out_shapes = [jax.ShapeDtypeStruct((1,) + w.shape[1:], BF16) for w, _ in cast]
    res = _resident if has_lat else pl.BlockSpec
    outs = pl.pallas_call(
        functools.partial(_attn_kernel, tq=tq, tk=tk, n_chunks=nk // tk if has_lat else 0, n_cast=len(cast)),
        grid=(b, KV_HEADS, nq),
        in_specs=[
            pl.BlockSpec(memory_space=pltpu.SMEM),
            pl.BlockSpec((None, tq, grp), lambda bi, h, i: (bi, i, h)),
            pl.BlockSpec((None, HEAD_DIM, m), lambda bi, h, i: (bi, h, 0)),
            pl.BlockSpec((None, m, 2 * HEAD_DIM), lambda bi, h, i: (bi, 0, h)),
            res((None, HEAD_DIM, nk), lambda bi, h, i: (bi, h, 0)),
            res((None, nk, 2 * HEAD_DIM), lambda bi, h, i: (bi, 0, h)),
        ] + cast_in_specs,
        out_specs=[pl.BlockSpec((None, tq, grp), lambda bi, h, i: (bi, i, h))] + cast_out_specs,
        out_shape=[jax.ShapeDtypeStruct((b, n, ATTN_WIDTH), BF16)] + cast_out_shapes,
        scratch_shapes=[
            pltpu.VMEM((2, GQA_GROUP * tq, tk), F32),
            pltpu.VMEM((GQA_GROUP * tq, 1), F32),
            pltpu.VMEM((GQA_GROUP * tq, 2 * HEAD_DIM), F32),
        ],
        compiler_params=pltpu.CompilerParams(
            dimension_semantics=("arbitrary", "arbitrary", "arbitrary"), vmem_limit_bytes=VMEM_LARGE),
        name="attention",
    )(bounded, q, kct, vxc, kt, vx, *[w for w, _ in cast])
    return outs[0], tuple(outs[1:])


def _mixout_kernel(x_ref, att_ref, pu_ref, pu_prev_ref, pu_next_ref, z_ref, z_prev_ref, z_next_ref, cb_ref,
                   mod_ref, g2_ref, wout_ref, wpool_ref, pscale_ref, convw_ref,
                   x1_ref, h2_ref, epu_ref, ez_ref, cat_ref, *, tm, n, n_sub):
    j = pl.program_id(1)
    first = j == 0
    last = j == pl.num_programs(1) - 1

    epu_ref[0:HALO, :] = jnp.where(first, 0.0, pu_prev_ref[...])
    epu_ref[HALO:HALO + tm, :] = pu_ref[...]
    epu_ref[HALO + tm:, :] = jnp.where(last, 0.0, pu_next_ref[...])
    ez_ref[0:HALO, :] = jnp.where(first, 0.0, z_prev_ref[...])
    ez_ref[HALO:HALO + tm, :] = z_ref[...]
    ez_ref[HALO + tm:, :] = jnp.where(last, 0.0, z_next_ref[...])

    cat_ref[:, 0:ATTN_WIDTH] = att_ref[...]

    rs = tm // n_sub
    for r in range(n_sub):
        r0 = r * rs
        pos = (j * tm + r0 + lax.broadcasted_iota(jnp.int32, (rs, 1), 0)).astype(F32)
        for gi, w in enumerate(POOL_WINDOWS):
            lo = w // 2
            hi = w - 1 - lo
            cols = slice(gi * POOL_GROUP, (gi + 1) * POOL_GROUP)
            tot = epu_ref[HALO + r0 - lo:HALO + r0 - lo + rs, cols]
            for o in range(-lo + 1, hi + 1):
                tot = tot + epu_ref[HALO + r0 + o:HALO + r0 + o + rs, cols]
            cnt = jnp.minimum(pos + (hi + 1), float(n)) - jnp.maximum(pos - lo, 0.0)
            d = tot / cnt - pu_ref[r0:r0 + rs, cols]
            yg = jnp.dot(d.astype(BF16), wpool_ref[gi], preferred_element_type=F32) * pscale_ref[:, cols]
            cat_ref[r0:r0 + rs, ATTN_WIDTH + gi * POOL_GROUP:ATTN_WIDTH + (gi + 1) * POOL_GROUP] = yg.astype(BF16)

        conv = (ez_ref[HALO + r0 - 1:HALO + r0 - 1 + rs, :] * convw_ref[0:1, :]
                + z_ref[r0:r0 + rs, :] * convw_ref[1:2, :]
                + ez_ref[HALO + r0 + 1:HALO + r0 + 1 + rs, :] * convw_ref[2:3, :])
        cat_ref[r0:r0 + rs, ATTN_WIDTH + POOL_WIDTH:] = (cb_ref[r0:r0 + rs, :] * conv).astype(BF16)

    gt1 = mod_ref[:, 2 * D_MODEL:3 * D_MODEL]
    sh2 = mod_ref[:, 3 * D_MODEL:4 * D_MODEL]
    gain2 = g2_ref[...] * (1.0 + mod_ref[:, 4 * D_MODEL:5 * D_MODEL])
    for r in range(n_sub):
        rows = slice(r * rs, (r + 1) * rs)
        x1 = x_ref[rows, :] + gt1 * jnp.dot(cat_ref[rows, :], wout_ref[...], preferred_element_type=F32)
        x1_ref[rows, :] = x1
        rinv = lax.rsqrt(jnp.mean(x1 * x1, axis=-1, keepdims=True) + EPS)
        h2_ref[rows, :] = (x1 * rinv * gain2 + sh2).astype(BF16)


def _mix_out(x, att, pu, z, cb, mod, g2, w_out, w_pool, pool_scale, conv_w, *, tm):
    b, n, _ = x.shape
    hb = tm // HALO
    nhb = n // HALO
    row = lambda width: pl.BlockSpec((None, tm, width), lambda bi, i: (bi, i, 0))
    prev = lambda width: pl.BlockSpec((None, HALO, width), lambda bi, i: (bi, jnp.maximum(i * hb - 1, 0), 0))
    nxt = lambda width: pl.BlockSpec((None, HALO, width), lambda bi, i: (bi, jnp.minimum((i + 1) * hb, nhb - 1), 0))
    vec = lambda width: pl.BlockSpec((1, width), lambda bi, i: (0, 0))
    return pl.pallas_call(
        functools.partial(_mixout_kernel, tm=tm, n=n, n_sub=max(1, tm // MIX_SUB_ROWS)),
        grid=(b, n // tm),
        in_specs=[
            row(D_MODEL), row(ATTN_WIDTH),
            row(POOL_WIDTH), prev(POOL_WIDTH), nxt(POOL_WIDTH),
            row(CONV_WIDTH), prev(CONV_WIDTH), nxt(CONV_WIDTH),
            row(CONV_WIDTH),
            pl.BlockSpec((None, 1, 6 * D_MODEL), lambda bi, i: (bi, 0, 0)),
            vec(D_MODEL),
            _resident((None, D_MODEL, D_MODEL), lambda bi, i: (0, 0, 0)),
            _resident((None, len(POOL_WINDOWS), POOL_GROUP, POOL_GROUP), lambda bi, i: (0, 0, 0, 0)),
            vec(POOL_WIDTH),
            pl.BlockSpec((CONV_K, CONV_WIDTH), lambda bi, i: (0, 0)),
        ],
        out_specs=(row(D_MODEL), row(D_MODEL)),
        out_shape=(jax.ShapeDtypeStruct((b, n, D_MODEL), F32), jax.ShapeDtypeStruct((b, n, D_MODEL), BF16)),
        scratch_shapes=[
            pltpu.VMEM((tm + 2 * HALO, POOL_WIDTH), F32),
            pltpu.VMEM((tm + 2 * HALO, CONV_WIDTH), F32),
            pltpu.VMEM((tm, D_MODEL), BF16),
        ],
        compiler_params=pltpu.CompilerParams(
            dimension_semantics=("arbitrary", "arbitrary"), vmem_limit_bytes=VMEM_MEDIUM),
        name="mix_out",
    )(x, att, pu, pu, pu, z, z, z, cb, mod, g2, w_out, w_pool, pool_scale, conv_w)


def _mlp_kernel(h_ref, x_hbm, mod_ref, w1_ref, w2_ref, o_ref, xbuf, xsem, *, tm):
    bi = pl.program_id(0)
    i = pl.program_id(1)
    f = pl.program_id(2)
    nf = pl.num_programs(2)

    def residual_copy():
        return pltpu.make_async_copy(x_hbm.at[bi, pl.ds(i * tm, tm), :], xbuf, xsem)

    @pl.when(f == 0)
    def _():
        residual_copy().start()
        o_ref[...] = jnp.zeros(o_ref.shape, F32)

    u = jnp.maximum(jnp.dot(h_ref[...], w1_ref[...], preferred_element_type=F32), 0.0)
    o_ref[...] += jnp.dot((u * u).astype(BF16), w2_ref[...], preferred_element_type=F32)

    @pl.when(f == nf - 1)
    def _():
        residual_copy().wait()
        gt2 = mod_ref[:, 5 * D_MODEL:6 * D_MODEL]
        o_ref[...] = xbuf[...] + gt2 * o_ref[...]


def _mlp(h2, x1, mod, w1, w2, *, tm, tf):
    b, n, _ = x1.shape
    return pl.pallas_call(
        functools.partial(_mlp_kernel, tm=tm),
        grid=(b, n // tm, D_FF // tf),
        in_specs=[
            pl.BlockSpec((None, tm, D_MODEL), lambda bi, i, f: (bi, i, 0)),
            pl.BlockSpec(memory_space=pl.ANY),
            pl.BlockSpec((None, 1, 6 * D_MODEL), lambda bi, i, f: (bi, 0, 0)),
            pl.BlockSpec((None, D_MODEL, tf), lambda bi, i, f: (0, 0, f)),
            pl.BlockSpec((None, tf, D_MODEL), lambda bi, i, f: (0, f, 0)),
        ],
        out_specs=pl.BlockSpec((None, tm, D_MODEL), lambda bi, i, f: (bi, i, 0)),
        out_shape=jax.ShapeDtypeStruct((b, n, D_MODEL), F32),
        scratch_shapes=[pltpu.VMEM((tm, D_MODEL), F32), pltpu.SemaphoreType.DMA(())],
        compiler_params=pltpu.CompilerParams(
            dimension_semantics=("arbitrary", "arbitrary", "arbitrary"), vmem_limit_bytes=VMEM_LARGE),
        name="mlp",
    )(h2, x1, mod, w1, w2)


def _rope_tables(n):
    rows = n // GRID_W
    n_freq = HEAD_DIM // 4
    inv = ROPE_BASE ** (-jnp.arange(n_freq, dtype=F32) / n_freq)
    ang_r = jnp.arange(rows, dtype=F32)[:, None] * inv
    ang_c = jnp.arange(GRID_W, dtype=F32)[:, None] * inv

    def table(fn):
        tr = jnp.repeat(fn(ang_r), GRID_W, axis=0)
        tc = jnp.tile(fn(ang_c), (rows, 1))
        return jnp.concatenate([tr, tr, tc, tc], axis=-1)

    cos, sin = table(jnp.cos), table(jnp.sin)
    odd_quarter = (jnp.arange(HEAD_DIM) // (HEAD_DIM // 4)) % 2 == 1
    sa = jnp.where(odd_quarter, 0.0, -sin)
    sb = jnp.where(odd_quarter, sin, 0.0)
    return cos, sa, sb


def kernel(x, c, ctx, c_ctx, w_mod, b_mod, norm1_g, norm2_g, w_in, q_norm_g, k_norm_g, w_pool, pool_scale,
           conv_w, w_out, w_ff1, w_ff2):
    b, n, _ = x.shape
    m = ctx.shape[1]
    depth = w_mod.shape[0]
    assert b + 1 <= MOD_ROWS

    cos, sa, sb = _rope_tables(n)
    cond = jnp.concatenate([c, c_ctx[None, :], jnp.zeros((MOD_ROWS - b - 1, D_MODEL), F32)], axis=0)
    mod_all = _modulation(cond, w_mod, b_mod)

    w_in_b = w_in[0:1].astype(BF16)
    w_pool_b = w_pool.astype(BF16)

    tm_ctx = m
    xc = ctx
    for l in range(depth):
        last = l == depth - 1
        mod = mod_all[l, :b][:, None, :]
        modc = jnp.broadcast_to(mod_all[l, b][None, None, :], (b, 1, 6 * D_MODEL))
        g1 = norm1_g[l][None, :]
        g2 = norm2_g[l][None, :]
        qg = q_norm_g[l][None, :]
        kg = k_norm_g[l][None, :]
        ps = pool_scale[l][None, :]
        wp = w_pool_b[l][None]

        qc, kct, vxc, puc, cbc, zc = _in_projection(xc, modc, g1, w_in_b, qg, kg, cos, sa, sb,
                                                    tm=tm_ctx, rope=False)
        q, kt, vx, pu, cb, z = _in_projection(x, mod, g1, w_in_b, qg, kg, cos, sa, sb, tm=ROW_TILE, rope=True)
        cast = [(w_out, l), (w_ff1, l), (w_ff2, l)] + ([] if last else [(w_in, l + 1)])
        score_bound = HEAD_DIM * jnp.max(jnp.abs(qg)) * jnp.max(jnp.abs(kg)) * (Q_SCALE_LOG2 * BF16_ROUNDING_SLACK)
        bounded = (score_bound <= SCORE_BOUND_LOG2).astype(jnp.int32).reshape(1)
        att, w_b = _attention(q, kct, vxc, kt, vx, bounded, tq=ATTN_TQ, tk=ATTN_TK, cast=cast)
        w_out_b, w1_b, w2_b = w_b[:3]
        x1, h2 = _mix_out(x, att, pu, z, cb, mod, g2, w_out_b, wp, ps, conv_w[l], tm=ROW_TILE)
        x = _mlp(h2, x1, mod, w1_b, w2_b, tm=MLP_TM, tf=MLP_TF)

        if not last:
            attc, _ = _attention(qc, kct, vxc, None, None, bounded, tq=HEAD_DIM, tk=HEAD_DIM)
            xc1, hc2 = _mix_out(xc, attc, puc, zc, cbc, modc, g2, w_out_b, wp, ps, conv_w[l], tm=tm_ctx)
            xc = _mlp(hc2.reshape(1, b * m, D_MODEL), xc1.reshape(1, b * m, D_MODEL), modc[:1], w1_b, w2_b,
                      tm=b * m, tf=MLP_TF).reshape(b, m, D_MODEL)
            w_in_b = w_b[3]
    return x
```

```python
import functools
import math

import jax
import jax.numpy as jnp
from jax import lax
from jax.experimental import pallas as pl
from jax.experimental.pallas import tpu as pltpu

D_MODEL = 2048
GRID_W = 64
HEAD_DIM = 128
ATTN_WIDTH = D_MODEL // 2
ATTN_HEADS = ATTN_WIDTH // HEAD_DIM
KV_HEADS = ATTN_HEADS // 4
GQA_GROUP = ATTN_HEADS // KV_HEADS
KV_WIDTH = KV_HEADS * HEAD_DIM
POOL_WIDTH = D_MODEL // 4
POOL_WINDOWS = (2, 4, 8, 16)
POOL_GROUP = POOL_WIDTH // len(POOL_WINDOWS)
CONV_WIDTH = D_MODEL // 4
CONV_K = 3
D_FF = 4 * D_MODEL
ROPE_BASE = 10000.0
EPS = 1e-6
ATTN_SCALE = 1.0 / math.sqrt(HEAD_DIM)
Q_SCALE_LOG2 = ATTN_SCALE * math.log2(math.e)

K_OFF = ATTN_WIDTH
V_OFF = K_OFF + KV_WIDTH
POOL_OFF = V_OFF + KV_WIDTH
CB_OFF = POOL_OFF + POOL_WIDTH
CC_OFF = CB_OFF + CONV_WIDTH
CV_OFF = CC_OFF + CONV_WIDTH
IN_WIDTH = CV_OFF + CONV_WIDTH

BF16_SUBLANES = 16
V_EXT_ROWS = HEAD_DIM + BF16_SUBLANES
HALO = 8
MOD_ROWS = 8
MIB = 1024 * 1024

MOD_TN = 1024
ROW_TILE = 512
MIX_SUB_ROWS = 256
ATTN_TQ = 512
ATTN_TK = 1024
BOUNDED_UNROLL = 4
SCORE_BOUND_LOG2 = 60.0
BF16_ROUNDING_SLACK = 1.02
MLP_TM = 1024
MLP_TF = 1024
VMEM_SMALL = 40 * MIB
VMEM_MEDIUM = 56 * MIB
VMEM_LARGE = 58 * MIB

F32 = jnp.float32
BF16 = jnp.bfloat16


def _resident(block_shape, index_map):
    return pl.BlockSpec(block_shape, index_map, pipeline_mode=pl.Buffered(1))


def _mod_kernel(a_ref, w_ref, b_ref, o_ref):
    a = a_ref[...]
    act = a / (1.0 + jnp.exp(-a))
    o_ref[...] = jnp.dot(act.astype(BF16), w_ref[...].astype(BF16),
                         preferred_element_type=F32) + b_ref[...]


def _modulation(cond, w_mod, b_mod):
    depth = w_mod.shape[0]
    return pl.pallas_call(
        _mod_kernel,
        grid=(depth, 6 * D_MODEL // MOD_TN),
        in_specs=[
            pl.BlockSpec((MOD_ROWS, D_MODEL), lambda l, j: (0, 0)),
            pl.BlockSpec((None, D_MODEL, MOD_TN), lambda l, j: (l, 0, j)),
            pl.BlockSpec((None, 1, MOD_TN), lambda l, j: (l, 0, j)),
        ],
        out_specs=pl.BlockSpec((None, MOD_ROWS, MOD_TN), lambda l, j: (l, 0, j)),
        out_shape=jax.ShapeDtypeStruct((depth, MOD_ROWS, 6 * D_MODEL), F32),
        compiler_params=pltpu.CompilerParams(
            dimension_semantics=("arbitrary", "arbitrary"), vmem_limit_bytes=VMEM_SMALL),
        name="modulation",
    )(cond, w_mod, b_mod.reshape(depth, 1, 6 * D_MODEL))


def _head_rmsnorm(x, g):
    return x * lax.rsqrt(jnp.mean(x * x, axis=-1, keepdims=True) + EPS) * g


def _inproj_kernel(x_ref, mod_ref, g1_ref, w_ref, qg_ref, kg_ref, cos_ref, sa_ref, sb_ref,
                   q_ref, k_ref, vt_ref, pu_ref, cb_ref, z_ref, *, rope):
    x = x_ref[...]
    rinv = lax.rsqrt(jnp.mean(x * x, axis=-1, keepdims=True) + EPS)
    sh = mod_ref[:, 0:D_MODEL]
    gain = g1_ref[...] * (1.0 + mod_ref[:, D_MODEL:2 * D_MODEL])
    h = (x * rinv * gain + sh).astype(BF16)

    def proj(lo, width):
        return jnp.dot(h, w_ref[:, lo:lo + width], preferred_element_type=F32)

    def rotary(t):
        if not rope:
            return t
        return (t * cos_ref[...] + pltpu.roll(t, 3 * HEAD_DIM // 4, axis=1) * sa_ref[...]
                + pltpu.roll(t, HEAD_DIM // 4, axis=1) * sb_ref[...])

    pq = proj(0, ATTN_WIDTH)
    for hh in range(ATTN_HEADS):
        t = _head_rmsnorm(pq[:, hh * HEAD_DIM:(hh + 1) * HEAD_DIM], qg_ref[...])
        q_ref[:, hh * HEAD_DIM:(hh + 1) * HEAD_DIM] = (rotary(t) * Q_SCALE_LOG2).astype(BF16)

    pk = proj(K_OFF, KV_WIDTH)
    for hh in range(KV_HEADS):
        t = rotary(_head_rmsnorm(pk[:, hh * HEAD_DIM:(hh + 1) * HEAD_DIM], kg_ref[...]))
        k_ref[:, hh * HEAD_DIM:(hh + 1) * HEAD_DIM] = t.astype(BF16)

    pv = proj(V_OFF, KV_WIDTH)
    for hh in range(KV_HEADS):
        vt_ref[hh * V_EXT_ROWS:hh * V_EXT_ROWS + HEAD_DIM, :] = pv[:, hh * HEAD_DIM:(hh + 1) * HEAD_DIM].T.astype(BF16)
        vt_ref[hh * V_EXT_ROWS + HEAD_DIM:(hh + 1) * V_EXT_ROWS, :] = jnp.ones((V_EXT_ROWS - HEAD_DIM, x.shape[0]), BF16)
    pu_ref[...] = proj(POOL_OFF, POOL_WIDTH)
    cb_ref[...] = proj(CB_OFF, CONV_WIDTH)
    z_ref[...] = proj(CC_OFF, CONV_WIDTH) * proj(CV_OFF, CONV_WIDTH)


def _in_projection(x, mod, g1, w_in, qg, kg, cos, sa, sb, *, tm, rope):
    b, n, _ = x.shape
    row = lambda width: pl.BlockSpec((None, tm, width), lambda bi, i: (bi, i, 0))
    tab = pl.BlockSpec((tm, HEAD_DIM), lambda bi, i: (i, 0))
    vec = lambda width: pl.BlockSpec((1, width), lambda bi, i: (0, 0))
    out_shape = (
        jax.ShapeDtypeStruct((b, n, ATTN_WIDTH), BF16),
        jax.ShapeDtypeStruct((b, n, KV_WIDTH), BF16),
        jax.ShapeDtypeStruct((b, KV_HEADS * V_EXT_ROWS, n), BF16),
        jax.ShapeDtypeStruct((b, n, POOL_WIDTH), F32),
        jax.ShapeDtypeStruct((b, n, CONV_WIDTH), F32),
        jax.ShapeDtypeStruct((b, n, CONV_WIDTH), F32),
    )
    return pl.pallas_call(
        functools.partial(_inproj_kernel, rope=rope),
        grid=(b, n // tm),
        in_specs=[
            row(D_MODEL),
            pl.BlockSpec((None, 1, 6 * D_MODEL), lambda bi, i: (bi, 0, 0)),
            vec(D_MODEL),
            _resident((None, D_MODEL, IN_WIDTH), lambda bi, i: (0, 0, 0)),
            vec(HEAD_DIM), vec(HEAD_DIM),
            tab, tab, tab,
        ],
        out_specs=(
            row(ATTN_WIDTH),
            row(KV_WIDTH),
            pl.BlockSpec((None, KV_HEADS * V_EXT_ROWS, tm), lambda bi, i: (bi, 0, i)),
            row(POOL_WIDTH), row(CONV_WIDTH), row(CONV_WIDTH),
        ),
        out_shape=out_shape,
        compiler_params=pltpu.CompilerParams(
            dimension_semantics=("arbitrary", "arbitrary"), vmem_limit_bytes=VMEM_MEDIUM),
        name="in_projection",
    )(x, mod, g1, w_in, qg, kg, cos, sa, sb)


def _attn_kernel(*refs, tq, tk, n_chunks, n_cast):
    bounded_ref, q_ref, kc_ref, vtc_ref, k_ref, vt_ref = refs[:6]
    cast_src = refs[6:6 + n_cast]
    o_ref = refs[6 + n_cast]
    cast_dst = refs[7 + n_cast:7 + 2 * n_cast]
    s_ref, m_ref, acc_ref = refs[7 + 2 * n_cast:]
    for src, dst in zip(cast_src, cast_dst):
        dst[...] = src[...].astype(BF16)

    qs = jnp.concatenate([q_ref[:, g * HEAD_DIM:(g + 1) * HEAD_DIM] for g in range(GQA_GROUP)], axis=0)

    def scores_t(k):
        return lax.dot_general(k, qs, (((1,), (1,)), ((), ())), preferred_element_type=F32)

    def write_out():
        acc = acc_ref[...]
        out = (acc[0:HEAD_DIM, :] / acc[HEAD_DIM:HEAD_DIM + 1, :]).T.astype(o_ref.dtype)
        for g in range(GQA_GROUP):
            o_ref[:, g * HEAD_DIM:(g + 1) * HEAD_DIM] = out[g * tq:(g + 1) * tq]

    @pl.when(bounded_ref[0] != 0)
    def _():
        def weighted_values_t(k, vt):
            return jnp.dot(vt, jnp.exp2(scores_t(k)).astype(BF16), preferred_element_type=F32)

        acc_ref[...] = weighted_values_t(kc_ref[...], vtc_ref[...])

        def trip(i, carry):
            for u in range(BOUNDED_UNROLL):
                c = i * BOUNDED_UNROLL + u
                acc_ref[...] += weighted_values_t(k_ref[pl.ds(c * tk, tk), :], vt_ref[:, pl.ds(c * tk, tk)])
            return carry

        if n_chunks:
            lax.fori_loop(0, n_chunks // BOUNDED_UNROLL, trip, 0)
        write_out()

    @pl.when(bounded_ref[0] == 0)
    def _():
        _online_softmax_t(scores_t, kc_ref, vtc_ref, k_ref, vt_ref, s_ref, m_ref, acc_ref, tk=tk, n_chunks=n_chunks)
        write_out()


def _online_softmax_t(scores_t, kc_ref, vtc_ref, k_ref, vt_ref, s_ref, m_ref, acc_ref, *, tk, n_chunks):
    def scores(c, slot):
        s_ref[slot] = scores_t(k_ref[pl.ds(c * tk, tk), :])

    def update(c, slot):
        s = s_ref[slot]
        m = m_ref[...]
        m_new = jnp.maximum(m, jnp.max(s, axis=0, keepdims=True))
        p = jnp.exp2(s - m_new).astype(BF16)
        pv = jnp.dot(vt_ref[:, pl.ds(c * tk, tk)], p, preferred_element_type=F32)
        acc_ref[...] = jnp.exp2(m - m_new) * acc_ref[...] + pv
        m_ref[...] = m_new

    sc = scores_t(kc_ref[...])
    if n_chunks:
        scores(0, 0)
    m0 = jnp.max(sc, axis=0, keepdims=True)
    m_ref[...] = m0
    acc_ref[...] = jnp.dot(vtc_ref[...], jnp.exp2(sc - m0).astype(BF16), preferred_element_type=F32)

    def pair(i, carry):
        c = 2 * i
        scores(c + 1, 1)
        update(c, 0)
        scores(c + 2, 0)
        update(c + 1, 1)
        return carry

    if n_chunks:
        n_pairs = (n_chunks - 1) // 2
        if n_pairs:
            lax.fori_loop(0, n_pairs, pair, 0)
        if n_chunks - 1 - 2 * n_pairs:
            scores(n_chunks - 1, 1)
            update(n_chunks - 2, 0)
            update(n_chunks - 1, 1)
        else:
            update(n_chunks - 1, 0)


def _attention(q, kc, vtc, k, vt, bounded, *, tq, tk, cast=()):
    b, n, _ = q.shape
    m = kc.shape[1]
    has_lat = k is not None
    if not has_lat:
        k, vt = kc, vtc
    nk = k.shape[1]
    grp = GQA_GROUP * HEAD_DIM
    rows = GQA_GROUP * tq
    nq = n // tq
    n_steps = b * KV_HEADS * nq
    step = lambda bi, h, i: (bi * KV_HEADS + h) * nq + i

    def slab_spec(w, layer):
        return pl.BlockSpec((None, w.shape[1] // n_steps, w.shape[2]), lambda bi, h, i: (layer, step(bi, h, i), 0))

    cast_in_specs = [slab_spec(w, layer) for w, layer in cast]
    cast_out_specs = [slab_spec(w, 0) for w, _ in cast]
    cast_out_shapes = [jax.ShapeDtypeStruct((1,) + w.shape[1:], BF16) for w, _ in cast]
    res = _resident if has_lat else pl.BlockSpec
    outs = pl.pallas_call(
        functools.partial(_attn_kernel, tq=tq, tk=tk, n_chunks=nk // tk if has_lat else 0, n_cast=len(cast)),
        grid=(b, KV_HEADS, nq),
        in_specs=[
            pl.BlockSpec(memory_space=pltpu.SMEM),
            pl.BlockSpec((None, tq, grp), lambda bi, h, i: (bi, i, h)),
            pl.BlockSpec((None, m, HEAD_DIM), lambda bi, h, i: (bi, 0, h)),
            pl.BlockSpec((None, V_EXT_ROWS, m), lambda bi, h, i: (bi, h, 0)),
            res((None, nk, HEAD_DIM), lambda bi, h, i: (bi, 0, h)),
            res((None, V_EXT_ROWS, nk), lambda bi, h, i: (bi, h, 0)),
        ] + cast_in_specs,
        out_specs=[pl.BlockSpec((None, tq, grp), lambda bi, h, i: (bi, i, h))] + cast_out_specs,
        out_shape=[jax.ShapeDtypeStruct((b, n, ATTN_WIDTH), BF16)] + cast_out_shapes,
        scratch_shapes=[
            pltpu.VMEM((2, tk, rows), F32),
            pltpu.VMEM((1, rows), F32),
            pltpu.VMEM((V_EXT_ROWS, rows), F32),
        ],
        compiler_params=pltpu.CompilerParams(
            dimension_semantics=("arbitrary", "arbitrary", "arbitrary"), vmem_limit_bytes=VMEM_LARGE),
        name="attention",
    )(bounded, q, kc, vtc, k, vt, *[w for w, _ in cast])
    return outs[0], tuple(outs[1:])


def _mixout_kernel(x_ref, att_ref, pu_ref, pu_prev_ref, pu_next_ref, z_ref, z_prev_ref, z_next_ref, cb_ref,
                   mod_ref, g2_ref, wout_ref, wpool_ref, pscale_ref, convw_ref,
                   x1_ref, h2_ref, epu_ref, ez_ref, cat_ref, *, tm, n, n_sub):
    j = pl.program_id(1)
    first = j == 0
    last = j == pl.num_programs(1) - 1

    epu_ref[0:HALO, :] = jnp.where(first, 0.0, pu_prev_ref[...])
    epu_ref[HALO:HALO + tm, :] = pu_ref[...]
    epu_ref[HALO + tm:, :] = jnp.where(last, 0.0, pu_next_ref[...])
    ez_ref[0:HALO, :] = jnp.where(first, 0.0, z_prev_ref[...])
    ez_ref[HALO:HALO + tm, :] = z_ref[...]
    ez_ref[HALO + tm:, :] = jnp.where(last, 0.0, z_next_ref[...])

    cat_ref[:, 0:ATTN_WIDTH] = att_ref[...]

    rs = tm // n_sub
    for r in range(n_sub):
        r0 = r * rs
        pos = (j * tm + r0 + lax.broadcasted_iota(jnp.int32, (rs, 1), 0)).astype(F32)
        for gi, w in enumerate(POOL_WINDOWS):
            lo = w // 2
            hi = w - 1 - lo
            cols = slice(gi * POOL_GROUP, (gi + 1) * POOL_GROUP)
            tot = epu_ref[HALO + r0 - lo:HALO + r0 - lo + rs, cols]
            for o in range(-lo + 1, hi + 1):
                tot = tot + epu_ref[HALO + r0 + o:HALO + r0 + o + rs, cols]
            cnt = jnp.minimum(pos + (hi + 1), float(n)) - jnp.maximum(pos - lo, 0.0)
            d = tot / cnt - pu_ref[r0:r0 + rs, cols]
            yg = jnp.dot(d.astype(BF16), wpool_ref[gi], preferred_element_type=F32) * pscale_ref[:, cols]
            cat_ref[r0:r0 + rs, ATTN_WIDTH + gi * POOL_GROUP:ATTN_WIDTH + (gi + 1) * POOL_GROUP] = yg.astype(BF16)

        conv = (ez_ref[HALO + r0 - 1:HALO + r0 - 1 + rs, :] * convw_ref[0:1, :]
                + z_ref[r0:r0 + rs, :] * convw_ref[1:2, :]
                + ez_ref[HALO + r0 + 1:HALO + r0 + 1 + rs, :] * convw_ref[2:3, :])
        cat_ref[r0:r0 + rs, ATTN_WIDTH + POOL_WIDTH:] = (cb_ref[r0:r0 + rs, :] * conv).astype(BF16)

    gt1 = mod_ref[:, 2 * D_MODEL:3 * D_MODEL]
    sh2 = mod_ref[:, 3 * D_MODEL:4 * D_MODEL]
    gain2 = g2_ref[...] * (1.0 + mod_ref[:, 4 * D_MODEL:5 * D_MODEL])
    for r in range(n_sub):
        rows = slice(r * rs, (r + 1) * rs)
        x1 = x_ref[rows, :] + gt1 * jnp.dot(cat_ref[rows, :], wout_ref[...], preferred_element_type=F32)
        x1_ref[rows, :] = x1
        rinv = lax.rsqrt(jnp.mean(x1 * x1, axis=-1, keepdims=True) + EPS)
        h2_ref[rows, :] = (x1 * rinv * gain2 + sh2).astype(BF16)


def _mix_out(x, att, pu, z, cb, mod, g2, w_out, w_pool, pool_scale, conv_w, *, tm):
    b, n, _ = x.shape
    hb = tm // HALO
    nhb = n // HALO
    row = lambda width: pl.BlockSpec((None, tm, width), lambda bi, i: (bi, i, 0))
    prev = lambda width: pl.BlockSpec((None, HALO, width), lambda bi, i: (bi, jnp.maximum(i * hb - 1, 0), 0))
    nxt = lambda width: pl.BlockSpec((None, HALO, width), lambda bi, i: (bi, jnp.minimum((i + 1) * hb, nhb - 1), 0))
    vec = lambda width: pl.BlockSpec((1, width), lambda bi, i: (0, 0))
    return pl.pallas_call(
        functools.partial(_mixout_kernel, tm=tm, n=n, n_sub=max(1, tm // MIX_SUB_ROWS)),
        grid=(b, n // tm),
        in_specs=[
            row(D_MODEL), row(ATTN_WIDTH),
            row(POOL_WIDTH), prev(POOL_WIDTH), nxt(POOL_WIDTH),
            row(CONV_WIDTH), prev(CONV_WIDTH), nxt(CONV_WIDTH),
            row(CONV_WIDTH),
            pl.BlockSpec((None, 1, 6 * D_MODEL), lambda bi, i: (bi, 0, 0)),
            vec(D_MODEL),
            _resident((None, D_MODEL, D_MODEL), lambda bi, i: (0, 0, 0)),
            _resident((None, len(POOL_WINDOWS), POOL_GROUP, POOL_GROUP), lambda bi, i: (0, 0, 0, 0)),
            vec(POOL_WIDTH),
            pl.BlockSpec((CONV_K, CONV_WIDTH), lambda bi, i: (0, 0)),
        ],
        out_specs=(row(D_MODEL), row(D_MODEL)),
        out_shape=(jax.ShapeDtypeStruct((b, n, D_MODEL), F32), jax.ShapeDtypeStruct((b, n, D_MODEL), BF16)),
        scratch_shapes=[
            pltpu.VMEM((tm + 2 * HALO, POOL_WIDTH), F32),
            pltpu.VMEM((tm + 2 * HALO, CONV_WIDTH), F32),
            pltpu.VMEM((tm, D_MODEL), BF16),
        ],
        compiler_params=pltpu.CompilerParams(
            dimension_semantics=("arbitrary", "arbitrary"), vmem_limit_bytes=VMEM_MEDIUM),
        name="mix_out",
    )(x, att, pu, pu, pu, z, z, z, cb, mod, g2, w_out, w_pool, pool_scale, conv_w)


def _mlp_kernel(h_ref, x_hbm, mod_ref, w1_ref, w2_ref, o_ref, xbuf, xsem, *, tm):
    bi = pl.program_id(0)
    i = pl.program_id(1)
    f = pl.program_id(2)
    nf = pl.num_programs(2)

    def residual_copy():
        return pltpu.make_async_copy(x_hbm.at[bi, pl.ds(i * tm, tm), :], xbuf, xsem)

    @pl.when(f == 0)
    def _():
        residual_copy().start()
        o_ref[...] = jnp.zeros(o_ref.shape, F32)

    u = jnp.maximum(jnp.dot(h_ref[...], w1_ref[...], preferred_element_type=F32), 0.0)
    o_ref[...] += jnp.dot((u * u).astype(BF16), w2_ref[...], preferred_element_type=F32)

    @pl.when(f == nf - 1)
    def _():
        residual_copy().wait()
        gt2 = mod_ref[:, 5 * D_MODEL:6 * D_MODEL]
        o_ref[...] = xbuf[...] + gt2 * o_ref[...]


def _mlp(h2, x1, mod, w1, w2, *, tm, tf):
    b, n, _ = x1.shape
    return pl.pallas_call(
        functools.partial(_mlp_kernel, tm=tm),
        grid=(b, n // tm, D_FF // tf),
        in_specs=[
            pl.BlockSpec((None, tm, D_MODEL), lambda bi, i, f: (bi, i, 0)),
            pl.BlockSpec(memory_space=pl.ANY),
            pl.BlockSpec((None, 1, 6 * D_MODEL), lambda bi, i, f: (bi, 0, 0)),
            pl.BlockSpec((None, D_MODEL, tf), lambda bi, i, f: (0, 0, f)),
            pl.BlockSpec((None, tf, D_MODEL), lambda bi, i, f: (0, f, 0)),
        ],
        out_specs=pl.BlockSpec((None, tm, D_MODEL), lambda bi, i, f: (bi, i, 0)),
        out_shape=jax.ShapeDtypeStruct((b, n, D_MODEL), F32),
        scratch_shapes=[pltpu.VMEM((tm, D_MODEL), F32), pltpu.SemaphoreType.DMA(())],
        compiler_params=pltpu.CompilerParams(
            dimension_semantics=("arbitrary", "arbitrary", "arbitrary"), vmem_limit_bytes=VMEM_LARGE),
        name="mlp",
    )(h2, x1, mod, w1, w2)


def _rope_tables(n):
    rows = n // GRID_W
    n_freq = HEAD_DIM // 4
    inv = ROPE_BASE ** (-jnp.arange(n_freq, dtype=F32) / n_freq)
    ang_r = jnp.arange(rows, dtype=F32)[:, None] * inv
    ang_c = jnp.arange(GRID_W, dtype=F32)[:, None] * inv

    def table(fn):
        tr = jnp.repeat(fn(ang_r), GRID_W, axis=0)
        tc = jnp.tile(fn(ang_c), (rows, 1))
        return jnp.concatenate([tr, tr, tc, tc], axis=-1)

    cos, sin = table(jnp.cos), table(jnp.sin)
    odd_quarter = (jnp.arange(HEAD_DIM) // (HEAD_DIM // 4)) % 2 == 1
    sa = jnp.where(odd_quarter, 0.0, -sin)
    sb = jnp.where(odd_quarter, sin, 0.0)
    return cos, sa, sb


def kernel(x, c, ctx, c_ctx, w_mod, b_mod, norm1_g, norm2_g, w_in, q_norm_g, k_norm_g, w_pool, pool_scale,
           conv_w, w_out, w_ff1, w_ff2):
    b, n, _ = x.shape
    m = ctx.shape[1]
    depth = w_mod.shape[0]
    assert b + 1 <= MOD_ROWS

    cos, sa, sb = _rope_tables(n)
    cond = jnp.concatenate([c, c_ctx[None, :], jnp.zeros((MOD_ROWS - b - 1, D_MODEL), F32)], axis=0)
    mod_all = _modulation(cond, w_mod, b_mod)

    w_in_b = w_in[0:1].astype(BF16)
    w_pool_b = w_pool.astype(BF16)

    tm_ctx = m
    xc = ctx
    for l in range(depth):
        last = l == depth - 1
        mod = mod_all[l, :b][:, None, :]
        modc = jnp.broadcast_to(mod_all[l, b][None, None, :], (b, 1, 6 * D_MODEL))
        g1 = norm1_g[l][None, :]
        g2 = norm2_g[l][None, :]
        qg = q_norm_g[l][None, :]
        kg = k_norm_g[l][None, :]
        ps = pool_scale[l][None, :]
        wp = w_pool_b[l][None]

        qc, kc, vtc, puc, cbc, zc = _in_projection(xc, modc, g1, w_in_b, qg, kg, cos, sa, sb,
                                                    tm=tm_ctx, rope=False)
        q, k, vt, pu, cb, z = _in_projection(x, mod, g1, w_in_b, qg, kg, cos, sa, sb, tm=ROW_TILE, rope=True)
        cast = [(w_out, l), (w_ff1, l), (w_ff2, l)] + ([] if last else [(w_in, l + 1)])
        score_bound = HEAD_DIM * jnp.max(jnp.abs(qg)) * jnp.max(jnp.abs(kg)) * (Q_SCALE_LOG2 * BF16_ROUNDING_SLACK)
        bounded = (score_bound <= SCORE_BOUND_LOG2).astype(jnp.int32).reshape(1)
        att, w_b = _attention(q, kc, vtc, k, vt, bounded, tq=ATTN_TQ, tk=ATTN_TK, cast=cast)
        w_out_b, w1_b, w2_b = w_b[:3]
        x1, h2 = _mix_out(x, att, pu, z, cb, mod, g2, w_out_b, wp, ps, conv_w[l], tm=ROW_TILE)
        x = _mlp(h2, x1, mod, w1_b, w2_b, tm=MLP_TM, tf=MLP_TF)

        if not last:
            attc, _ = _attention(qc, kc, vtc, None, None, bounded, tq=HEAD_DIM, tk=HEAD_DIM)
            xc1, hc2 = _mix_out(xc, attc, puc, zc, cbc, modc, g2, w_out_b, wp, ps, conv_w[l], tm=tm_ctx)
            xc = _mlp(hc2.reshape(1, b * m, D_MODEL), xc1.reshape(1, b * m, D_MODEL), modc[:1], w1_b, w2_b,
                      tm=b * m, tf=MLP_TF).reshape(b, m, D_MODEL)
            w_in_b = w_b[3]
    return x
```

```python
import functools
import math

import jax
import jax.numpy as jnp
from jax import lax
from jax.experimental import pallas as pl
from jax.experimental.pallas import tpu as pltpu

D_MODEL = 2048
GRID_W = 64
HEAD_DIM = 128
ATTN_WIDTH = D_MODEL // 2
ATTN_HEADS = ATTN_WIDTH // HEAD_DIM
KV_HEADS = ATTN_HEADS // 4
GQA_GROUP = ATTN_HEADS // KV_HEADS
KV_WIDTH = KV_HEADS * HEAD_DIM
POOL_WIDTH = D_MODEL // 4
POOL_WINDOWS = (2, 4, 8, 16)
POOL_GROUP = POOL_WIDTH // len(POOL_WINDOWS)
CONV_WIDTH = D_MODEL // 4
CONV_K = 3
D_FF = 4 * D_MODEL
ROPE_BASE = 10000.0
EPS = 1e-6
ATTN_SCALE = 1.0 / math.sqrt(HEAD_DIM)
Q_SCALE_LOG2 = ATTN_SCALE * math.log2(math.e)

K_OFF = ATTN_WIDTH
V_OFF = K_OFF + KV_WIDTH
POOL_OFF = V_OFF + KV_WIDTH
CB_OFF = POOL_OFF + POOL_WIDTH
CC_OFF = CB_OFF + CONV_WIDTH
CV_OFF = CC_OFF + CONV_WIDTH
IN_WIDTH = CV_OFF + CONV_WIDTH

HALO = 8
MOD_ROWS = 8
MIB = 1024 * 1024

MOD_TN = 1024
ROW_TILE = 512
MIX_SUB_ROWS = 256
ATTN_TQ = 512
ATTN_TK = 1024
BOUNDED_UNROLL = 8
SCORE_BOUND_LOG2 = 60.0
BF16_ROUNDING_SLACK = 1.02
MLP_TM = 1024
MLP_TF = 1024
VMEM_SMALL = 40 * MIB
VMEM_MEDIUM = 56 * MIB
VMEM_LARGE = 58 * MIB

F32 = jnp.float32
BF16 = jnp.bfloat16


def _resident(block_shape, index_map):
    return pl.BlockSpec(block_shape, index_map, pipeline_mode=pl.Buffered(1))


def _mod_kernel(a_ref, w_ref, b_ref, o_ref):
    a = a_ref[...]
    act = a / (1.0 + jnp.exp(-a))
    o_ref[...] = jnp.dot(act.astype(BF16), w_ref[...].astype(BF16),
                         preferred_element_type=F32) + b_ref[...]


def _modulation(cond, w_mod, b_mod):
    depth = w_mod.shape[0]
    return pl.pallas_call(
        _mod_kernel,
        grid=(depth, 6 * D_MODEL // MOD_TN),
        in_specs=[
            pl.BlockSpec((MOD_ROWS, D_MODEL), lambda l, j: (0, 0)),
            pl.BlockSpec((None, D_MODEL, MOD_TN), lambda l, j: (l, 0, j)),
            pl.BlockSpec((None, 1, MOD_TN), lambda l, j: (l, 0, j)),
        ],
        out_specs=pl.BlockSpec((None, MOD_ROWS, MOD_TN), lambda l, j: (l, 0, j)),
        out_shape=jax.ShapeDtypeStruct((depth, MOD_ROWS, 6 * D_MODEL), F32),
        compiler_params=pltpu.CompilerParams(
            dimension_semantics=("arbitrary", "arbitrary"), vmem_limit_bytes=VMEM_SMALL),
        name="modulation",
    )(cond, w_mod, b_mod.reshape(depth, 1, 6 * D_MODEL))


def _head_rmsnorm(x, g):
    return x * lax.rsqrt(jnp.mean(x * x, axis=-1, keepdims=True) + EPS) * g


def _inproj_kernel(x_ref, mod_ref, g1_ref, w_ref, qg_ref, kg_ref, cos_ref, sa_ref, sb_ref,
                   q_ref, k_ref, vt_ref, pu_ref, cb_ref, z_ref, *, rope):
    x = x_ref[...]
    rinv = lax.rsqrt(jnp.mean(x * x, axis=-1, keepdims=True) + EPS)
    sh = mod_ref[:, 0:D_MODEL]
    gain = g1_ref[...] * (1.0 + mod_ref[:, D_MODEL:2 * D_MODEL])
    h = (x * rinv * gain + sh).astype(BF16)

    def proj(lo, width):
        return jnp.dot(h, w_ref[:, lo:lo + width], preferred_element_type=F32)

    def rotary(t):
        if not rope:
            return t
        return (t * cos_ref[...] + pltpu.roll(t, 3 * HEAD_DIM // 4, axis=1) * sa_ref[...]
                + pltpu.roll(t, HEAD_DIM // 4, axis=1) * sb_ref[...])

    pq = proj(0, ATTN_WIDTH)
    for hh in range(ATTN_HEADS):
        t = _head_rmsnorm(pq[:, hh * HEAD_DIM:(hh + 1) * HEAD_DIM], qg_ref[...])
        q_ref[:, hh * HEAD_DIM:(hh + 1) * HEAD_DIM] = (rotary(t) * Q_SCALE_LOG2).astype(BF16)

    pk = proj(K_OFF, KV_WIDTH)
    for hh in range(KV_HEADS):
        t = rotary(_head_rmsnorm(pk[:, hh * HEAD_DIM:(hh + 1) * HEAD_DIM], kg_ref[...]))
        k_ref[:, hh * HEAD_DIM:(hh + 1) * HEAD_DIM] = t.astype(BF16)

    pv = proj(V_OFF, KV_WIDTH)
    for hh in range(KV_HEADS):
        vt_ref[hh * HEAD_DIM:(hh + 1) * HEAD_DIM, :] = pv[:, hh * HEAD_DIM:(hh + 1) * HEAD_DIM].T.astype(BF16)
    pu_ref[...] = proj(POOL_OFF, POOL_WIDTH)
    cb_ref[...] = proj(CB_OFF, CONV_WIDTH)
    z_ref[...] = proj(CC_OFF, CONV_WIDTH) * proj(CV_OFF, CONV_WIDTH)


def _in_projection(x, mod, g1, w_in, qg, kg, cos, sa, sb, *, tm, rope):
    b, n, _ = x.shape
    row = lambda width: pl.BlockSpec((None, tm, width), lambda bi, i: (bi, i, 0))
    tab = pl.BlockSpec((tm, HEAD_DIM), lambda bi, i: (i, 0))
    vec = lambda width: pl.BlockSpec((1, width), lambda bi, i: (0, 0))
    out_shape = (
        jax.ShapeDtypeStruct((b, n, ATTN_WIDTH), BF16),
        jax.ShapeDtypeStruct((b, n, KV_WIDTH), BF16),
        jax.ShapeDtypeStruct((b, KV_WIDTH, n), BF16),
        jax.ShapeDtypeStruct((b, n, POOL_WIDTH), F32),
        jax.ShapeDtypeStruct((b, n, CONV_WIDTH), F32),
        jax.ShapeDtypeStruct((b, n, CONV_WIDTH), F32),
    )
    return pl.pallas_call(
        functools.partial(_inproj_kernel, rope=rope),
        grid=(b, n // tm),
        in_specs=[
            row(D_MODEL),
            pl.BlockSpec((None, 1, 6 * D_MODEL), lambda bi, i: (bi, 0, 0)),
            vec(D_MODEL),
            _resident((None, D_MODEL, IN_WIDTH), lambda bi, i: (0, 0, 0)),
            vec(HEAD_DIM), vec(HEAD_DIM),
            tab, tab, tab,
        ],
        out_specs=(
            row(ATTN_WIDTH),
            row(KV_WIDTH),
            pl.BlockSpec((None, KV_WIDTH, tm), lambda bi, i: (bi, 0, i)),
            row(POOL_WIDTH), row(CONV_WIDTH), row(CONV_WIDTH),
        ),
        out_shape=out_shape,
        compiler_params=pltpu.CompilerParams(
            dimension_semantics=("arbitrary", "arbitrary"), vmem_limit_bytes=VMEM_MEDIUM),
        name="in_projection",
    )(x, mod, g1, w_in, qg, kg, cos, sa, sb)


def _attn_kernel(*refs, tq, tk, n_chunks, n_cast):
    bounded_ref, q_ref, kc_ref, vtc_ref, k_ref, vt_ref = refs[:6]
    cast_src = refs[6:6 + n_cast]
    o_ref = refs[6 + n_cast]
    cast_dst = refs[7 + n_cast:7 + 2 * n_cast]
    s_ref, m_ref, l_ref, acc_ref = refs[7 + 2 * n_cast:]
    for src, dst in zip(cast_src, cast_dst):
        dst[...] = src[...].astype(BF16)

    qs = jnp.concatenate([q_ref[:, g * HEAD_DIM:(g + 1) * HEAD_DIM] for g in range(GQA_GROUP)], axis=0)

    def scores_t(k):
        return lax.dot_general(k, qs, (((1,), (1,)), ((), ())), preferred_element_type=F32)

    def weigh(p, vt):
        return jnp.dot(vt, p.astype(BF16), preferred_element_type=F32), jnp.sum(p, axis=0, keepdims=True)

    def write_out():
        out = (acc_ref[...] / l_ref[...]).T.astype(o_ref.dtype)
        for g in range(GQA_GROUP):
            o_ref[:, g * HEAD_DIM:(g + 1) * HEAD_DIM] = out[g * tq:(g + 1) * tq]

    @pl.when(bounded_ref[0] != 0)
    def _():
        acc_ref[...], l_ref[...] = weigh(jnp.exp2(scores_t(kc_ref[...])), vtc_ref[...])

        def trip(i, carry):
            for u in range(BOUNDED_UNROLL):
                c = i * BOUNDED_UNROLL + u
                num, den = weigh(jnp.exp2(scores_t(k_ref[pl.ds(c * tk, tk), :])), vt_ref[:, pl.ds(c * tk, tk)])
                acc_ref[...] += num
                l_ref[...] += den
            return carry

        if n_chunks:
            lax.fori_loop(0, n_chunks // BOUNDED_UNROLL, trip, 0)
        write_out()

    @pl.when(bounded_ref[0] == 0)
    def _():
        _online_softmax_t(scores_t, weigh, kc_ref, vtc_ref, k_ref, vt_ref, s_ref, m_ref, l_ref, acc_ref,
                          tk=tk, n_chunks=n_chunks)
        write_out()


def _online_softmax_t(scores_t, weigh, kc_ref, vtc_ref, k_ref, vt_ref, s_ref, m_ref, l_ref, acc_ref, *, tk, n_chunks):
    def scores(c, slot):
        s_ref[slot] = scores_t(k_ref[pl.ds(c * tk, tk), :])

    def update(c, slot):
        s = s_ref[slot]
        m = m_ref[...]
        m_new = jnp.maximum(m, jnp.max(s, axis=0, keepdims=True))
        num, den = weigh(jnp.exp2(s - m_new), vt_ref[:, pl.ds(c * tk, tk)])
        alpha = jnp.exp2(m - m_new)
        acc_ref[...] = alpha * acc_ref[...] + num
        l_ref[...] = alpha * l_ref[...] + den
        m_ref[...] = m_new

    sc = scores_t(kc_ref[...])
    if n_chunks:
        scores(0, 0)
    m0 = jnp.max(sc, axis=0, keepdims=True)
    m_ref[...] = m0
    acc_ref[...], l_ref[...] = weigh(jnp.exp2(sc - m0), vtc_ref[...])

    def pair(i, carry):
        c = 2 * i
        scores(c + 1, 1)
        update(c, 0)
        scores(c + 2, 0)
        update(c + 1, 1)
        return carry

    if n_chunks:
        n_pairs = (n_chunks - 1) // 2
        if n_pairs:
            lax.fori_loop(0, n_pairs, pair, 0)
        if n_chunks - 1 - 2 * n_pairs:
            scores(n_chunks - 1, 1)
            update(n_chunks - 2, 0)
            update(n_chunks - 1, 1)
        else:
            update(n_chunks - 1, 0)


def _attention(q, kc, vtc, k, vt, bounded, *, tq, tk, cast=()):
    b, n, _ = q.shape
    m = kc.shape[1]
    has_lat = k is not None
    if not has_lat:
        k, vt = kc, vtc
    nk = k.shape[1]
    grp = GQA_GROUP * HEAD_DIM
    rows = GQA_GROUP * tq
    nq = n // tq
    n_steps = b * KV_HEADS * nq
    step = lambda bi, h, i: (bi * KV_HEADS + h) * nq + i

    def slab_spec(w, layer):
        return pl.BlockSpec((None, w.shape[1] // n_steps, w.shape[2]), lambda bi, h, i: (layer, step(bi, h, i), 0))

    cast_in_specs = [slab_spec(w, layer) for w, layer in cast]
    cast_out_specs = [slab_spec(w, 0) for w, _ in cast]
    cast_out_shapes = [jax.ShapeDtypeStruct((1,) + w.shape[1:], BF16) for w, _ in cast]
    res = _resident if has_lat else pl.BlockSpec
    outs = pl.pallas_call(
        functools.partial(_attn_kernel, tq=tq, tk=tk, n_chunks=nk // tk if has_lat else 0, n_cast=len(cast)),
        grid=(b, KV_HEADS, nq),
        in_specs=[
            pl.BlockSpec(memory_space=pltpu.SMEM),
            pl.BlockSpec((None, tq, grp), lambda bi, h, i: (bi, i, h)),
            pl.BlockSpec((None, m, HEAD_DIM), lambda bi, h, i: (bi, 0, h)),
            pl.BlockSpec((None, HEAD_DIM, m), lambda bi, h, i: (bi, h, 0)),
            res((None, nk, HEAD_DIM), lambda bi, h, i: (bi, 0, h)),
            res((None, HEAD_DIM, nk), lambda bi, h, i: (bi, h, 0)),
        ] + cast_in_specs,
        out_specs=[pl.BlockSpec((None, tq, grp), lambda bi, h, i: (bi, i, h))] + cast_out_specs,
        out_shape=[jax.ShapeDtypeStruct((b, n, ATTN_WIDTH), BF16)] + cast_out_shapes,
        scratch_shapes=[
            pltpu.VMEM((2, tk, rows), F32),
            pltpu.VMEM((1, rows), F32),
            pltpu.VMEM((1, rows), F32),
            pltpu.VMEM((HEAD_DIM, rows), F32),
        ],
        compiler_params=pltpu.CompilerParams(
            dimension_semantics=("arbitrary", "arbitrary", "arbitrary"), vmem_limit_bytes=VMEM_LARGE),
        name="attention",
    )(bounded, q, kc, vtc, k, vt, *[w for w, _ in cast])
    return outs[0], tuple(outs[1:])


def _mixout_kernel(x_ref, att_ref, pu_ref, pu_prev_ref, pu_next_ref, z_ref, z_prev_ref, z_next_ref, cb_ref,
                   mod_ref, g2_ref, wout_ref, wpool_ref, pscale_ref, convw_ref,
                   x1_ref, h2_ref, epu_ref, ez_ref, cat_ref, *, tm, n, n_sub):
    j = pl.program_id(1)
    first = j == 0
    last = j == pl.num_programs(1) - 1

    epu_ref[0:HALO, :] = jnp.where(first, 0.0, pu_prev_ref[...])
    epu_ref[HALO:HALO + tm, :] = pu_ref[...]
    epu_ref[HALO + tm:, :] = jnp.where(last, 0.0, pu_next_ref[...])
    ez_ref[0:HALO, :] = jnp.where(first, 0.0, z_prev_ref[...])
    ez_ref[HALO:HALO + tm, :] = z_ref[...]
    ez_ref[HALO + tm:, :] = jnp.where(last, 0.0, z_next_ref[...])

    cat_ref[:, 0:ATTN_WIDTH] = att_ref[...]

    rs = tm // n_sub
    for r in range(n_sub):
        r0 = r * rs
        pos = (j * tm + r0 + lax.broadcasted_iota(jnp.int32, (rs, 1), 0)).astype(F32)
        for gi, w in enumerate(POOL_WINDOWS):
            lo = w // 2
            hi = w - 1 - lo
            cols = slice(gi * POOL_GROUP, (gi + 1) * POOL_GROUP)
            tot = epu_ref[HALO + r0 - lo:HALO + r0 - lo + rs, cols]
            for o in range(-lo + 1, hi + 1):
                tot = tot + epu_ref[HALO + r0 + o:HALO + r0 + o + rs, cols]
            cnt = jnp.minimum(pos + (hi + 1), float(n)) - jnp.maximum(pos - lo, 0.0)
            d = tot / cnt - pu_ref[r0:r0 + rs, cols]
            yg = jnp.dot(d.astype(BF16), wpool_ref[gi], preferred_element_type=F32) * pscale_ref[:, cols]
            cat_ref[r0:r0 + rs, ATTN_WIDTH + gi * POOL_GROUP:ATTN_WIDTH + (gi + 1) * POOL_GROUP] = yg.astype(BF16)

        conv = (ez_ref[HALO + r0 - 1:HALO + r0 - 1 + rs, :] * convw_ref[0:1, :]
                + z_ref[r0:r0 + rs, :] * convw_ref[1:2, :]
                + ez_ref[HALO + r0 + 1:HALO + r0 + 1 + rs, :] * convw_ref[2:3, :])
        cat_ref[r0:r0 + rs, ATTN_WIDTH + POOL_WIDTH:] = (cb_ref[r0:r0 + rs, :] * conv).astype(BF16)

    gt1 = mod_ref[:, 2 * D_MODEL:3 * D_MODEL]
    sh2 = mod_ref[:, 3 * D_MODEL:4 * D_MODEL]
    gain2 = g2_ref[...] * (1.0 + mod_ref[:, 4 * D_MODEL:5 * D_MODEL])
    for r in range(n_sub):
        rows = slice(r * rs, (r + 1) * rs)
        x1 = x_ref[rows, :] + gt1 * jnp.dot(cat_ref[rows, :], wout_ref[...], preferred_element_type=F32)
        x1_ref[rows, :] = x1
        rinv = lax.rsqrt(jnp.mean(x1 * x1, axis=-1, keepdims=True) + EPS)
        h2_ref[rows, :] = (x1 * rinv * gain2 + sh2).astype(BF16)


def _mix_out(x, att, pu, z, cb, mod, g2, w_out, w_pool, pool_scale, conv_w, *, tm):
    b, n, _ = x.shape
    hb = tm // HALO
    nhb = n // HALO
    row = lambda width: pl.BlockSpec((None, tm, width), lambda bi, i: (bi, i, 0))
    prev = lambda width: pl.BlockSpec((None, HALO, width), lambda bi, i: (bi, jnp.maximum(i * hb - 1, 0), 0))
    nxt = lambda width: pl.BlockSpec((None, HALO, width), lambda bi, i: (bi, jnp.minimum((i + 1) * hb, nhb - 1), 0))
    vec = lambda width: pl.BlockSpec((1, width), lambda bi, i: (0, 0))
    return pl.pallas_call(
        functools.partial(_mixout_kernel, tm=tm, n=n, n_sub=max(1, tm // MIX_SUB_ROWS)),
        grid=(b, n // tm),
        in_specs=[
            row(D_MODEL), row(ATTN_WIDTH),
            row(POOL_WIDTH), prev(POOL_WIDTH), nxt(POOL_WIDTH),
            row(CONV_WIDTH), prev(CONV_WIDTH), nxt(CONV_WIDTH),
            row(CONV_WIDTH),
            pl.BlockSpec((None, 1, 6 * D_MODEL), lambda bi, i: (bi, 0, 0)),
            vec(D_MODEL),
            _resident((None, D_MODEL, D_MODEL), lambda bi, i: (0, 0, 0)),
            _resident((None, len(POOL_WINDOWS), POOL_GROUP, POOL_GROUP), lambda bi, i: (0, 0, 0, 0)),
            vec(POOL_WIDTH),
            pl.BlockSpec((CONV_K, CONV_WIDTH), lambda bi, i: (0, 0)),
        ],
        out_specs=(row(D_MODEL), row(D_MODEL)),
        out_shape=(jax.ShapeDtypeStruct((b, n, D_MODEL), F32), jax.ShapeDtypeStruct((b, n, D_MODEL), BF16)),
        scratch_shapes=[
            pltpu.VMEM((tm + 2 * HALO, POOL_WIDTH), F32),
            pltpu.VMEM((tm + 2 * HALO, CONV_WIDTH), F32),
            pltpu.VMEM((tm, D_MODEL), BF16),
        ],
        compiler_params=pltpu.CompilerParams(
            dimension_semantics=("arbitrary", "arbitrary"), vmem_limit_bytes=VMEM_MEDIUM),
        name="mix_out",
    )(x, att, pu, pu, pu, z, z, z, cb, mod, g2, w_out, w_pool, pool_scale, conv_w)


def _mlp_kernel(h_ref, x_hbm, mod_ref, w1_ref, w2_ref, o_ref, xbuf, xsem, *, tm):
    bi = pl.program_id(0)
    i = pl.program_id(1)
    f = pl.program_id(2)
    nf = pl.num_programs(2)

    def residual_copy():
        return pltpu.make_async_copy(x_hbm.at[bi, pl.ds(i * tm, tm), :], xbuf, xsem)

    @pl.when(f == 0)
    def _():
        residual_copy().start()
        o_ref[...] = jnp.zeros(o_ref.shape, F32)

    u = jnp.maximum(jnp.dot(h_ref[...], w1_ref[...], preferred_element_type=F32), 0.0)
    o_ref[...] += jnp.dot((u * u).astype(BF16), w2_ref[...], preferred_element_type=F32)

    @pl.when(f == nf - 1)
    def _():
        residual_copy().wait()
        gt2 = mod_ref[:, 5 * D_MODEL:6 * D_MODEL]
        o_ref[...] = xbuf[...] + gt2 * o_ref[...]


def _mlp(h2, x1, mod, w1, w2, *, tm, tf):
    b, n, _ = x1.shape
    return pl.pallas_call(
        functools.partial(_mlp_kernel, tm=tm),
        grid=(b, n // tm, D_FF // tf),
        in_specs=[
            pl.BlockSpec((None, tm, D_MODEL), lambda bi, i, f: (bi, i, 0)),
            pl.BlockSpec(memory_space=pl.ANY),
            pl.BlockSpec((None, 1, 6 * D_MODEL), lambda bi, i, f: (bi, 0, 0)),
            pl.BlockSpec((None, D_MODEL, tf), lambda bi, i, f: (0, 0, f)),
            pl.BlockSpec((None, tf, D_MODEL), lambda bi, i, f: (0, f, 0)),
        ],
        out_specs=pl.BlockSpec((None, tm, D_MODEL), lambda bi, i, f: (bi, i, 0)),
        out_shape=jax.ShapeDtypeStruct((b, n, D_MODEL), F32),
        scratch_shapes=[pltpu.VMEM((tm, D_MODEL), F32), pltpu.SemaphoreType.DMA(())],
        compiler_params=pltpu.CompilerParams(
            dimension_semantics=("arbitrary", "arbitrary", "arbitrary"), vmem_limit_bytes=VMEM_LARGE),
        name="mlp",
    )(h2, x1, mod, w1, w2)


def _rope_tables(n):
    rows = n // GRID_W
    n_freq = HEAD_DIM // 4
    inv = ROPE_BASE ** (-jnp.arange(n_freq, dtype=F32) / n_freq)
    ang_r = jnp.arange(rows, dtype=F32)[:, None] * inv
    ang_c = jnp.arange(GRID_W, dtype=F32)[:, None] * inv

    def table(fn):
        tr = jnp.repeat(fn(ang_r), GRID_W, axis=0)
        tc = jnp.tile(fn(ang_c), (rows, 1))
        return jnp.concatenate([tr, tr, tc, tc], axis=-1)

    cos, sin = table(jnp.cos), table(jnp.sin)
    odd_quarter = (jnp.arange(HEAD_DIM) // (HEAD_DIM // 4)) % 2 == 1
    sa = jnp.where(odd_quarter, 0.0, -sin)
    sb = jnp.where(odd_quarter, sin, 0.0)
    return cos, sa, sb


def kernel(x, c, ctx, c_ctx, w_mod, b_mod, norm1_g, norm2_g, w_in, q_norm_g, k_norm_g, w_pool, pool_scale,
           conv_w, w_out, w_ff1, w_ff2):
    b, n, _ = x.shape
    m = ctx.shape[1]
    depth = w_mod.shape[0]
    assert b + 1 <= MOD_ROWS

    cos, sa, sb = _rope_tables(n)
    cond = jnp.concatenate([c, c_ctx[None, :], jnp.zeros((MOD_ROWS - b - 1, D_MODEL), F32)], axis=0)
    mod_all = _modulation(cond, w_mod, b_mod)

    w_in_b = w_in[0:1].astype(BF16)
    w_pool_b = w_pool.astype(BF16)

    tm_ctx = m
    xc = ctx
    for l in range(depth):
        last = l == depth - 1
        mod = mod_all[l, :b][:, None, :]
        modc = jnp.broadcast_to(mod_all[l, b][None, None, :], (b, 1, 6 * D_MODEL))
        g1 = norm1_g[l][None, :]
        g2 = norm2_g[l][None, :]
        qg = q_norm_g[l][None, :]
        kg = k_norm_g[l][None, :]
        ps = pool_scale[l][None, :]
        wp = w_pool_b[l][None]

        qc, kc, vtc, puc, cbc, zc = _in_projection(xc, modc, g1, w_in_b, qg, kg, cos, sa, sb,
                                                    tm=tm_ctx, rope=False)
        q, k, vt, pu, cb, z = _in_projection(x, mod, g1, w_in_b, qg, kg, cos, sa, sb, tm=ROW_TILE, rope=True)
        cast = [(w_out, l), (w_ff1, l), (w_ff2, l)] + ([] if last else [(w_in, l + 1)])
        score_bound = HEAD_DIM * jnp.max(jnp.abs(qg)) * jnp.max(jnp.abs(kg)) * (Q_SCALE_LOG2 * BF16_ROUNDING_SLACK)
        bounded = (score_bound <= SCORE_BOUND_LOG2).astype(jnp.int32).reshape(1)
        att, w_b = _attention(q, kc, vtc, k, vt, bounded, tq=ATTN_TQ, tk=ATTN_TK, cast=cast)
        w_out_b, w1_b, w2_b = w_b[:3]
        x1, h2 = _mix_out(x, att, pu, z, cb, mod, g2, w_out_b, wp, ps, conv_w[l], tm=ROW_TILE)
        x = _mlp(h2, x1, mod, w1_b, w2_b, tm=MLP_TM, tf=MLP_TF)

        if not last:
            attc, _ = _attention(qc, kc, vtc, None, None, bounded, tq=HEAD_DIM, tk=HEAD_DIM)
            xc1, hc2 = _mix_out(xc, attc, puc, zc, cbc, modc, g2, w_out_b, wp, ps, conv_w[l], tm=tm_ctx)
            xc = _mlp(hc2.reshape(1, b * m, D_MODEL), xc1.reshape(1, b * m, D_MODEL), modc[:1], w1_b, w2_b,
                      tm=b * m, tf=MLP_TF).reshape(b, m, D_MODEL)
            w_in_b = w_b[3]
    return x
```

```python
import functools
import math

import jax
import jax.numpy as jnp
from jax import lax
from jax.experimental import pallas as pl
from jax.experimental.pallas import tpu as pltpu

D_MODEL = 2048
GRID_W = 64
HEAD_DIM = 128
ATTN_WIDTH = D_MODEL // 2
ATTN_HEADS = ATTN_WIDTH // HEAD_DIM
KV_HEADS = ATTN_HEADS // 4
GQA_GROUP = ATTN_HEADS // KV_HEADS
KV_WIDTH = KV_HEADS * HEAD_DIM
POOL_WIDTH = D_MODEL // 4
POOL_WINDOWS = (2, 4, 8, 16)
POOL_GROUP = POOL_WIDTH // len(POOL_WINDOWS)
CONV_WIDTH = D_MODEL // 4
CONV_K = 3
D_FF = 4 * D_MODEL
ROPE_BASE = 10000.0
EPS = 1e-6
ATTN_SCALE = 1.0 / math.sqrt(HEAD_DIM)
Q_SCALE_LOG2 = ATTN_SCALE * math.log2(math.e)

K_OFF = ATTN_WIDTH
V_OFF = K_OFF + KV_WIDTH
POOL_OFF = V_OFF + KV_WIDTH
CB_OFF = POOL_OFF + POOL_WIDTH
CC_OFF = CB_OFF + CONV_WIDTH
CV_OFF = CC_OFF + CONV_WIDTH
IN_WIDTH = CV_OFF + CONV_WIDTH

HALO = 8
MOD_ROWS = 8
MIB = 1024 * 1024

MOD_TN = 1024
ROW_TILE = 512
MIX_SUB_ROWS = 256
ATTN_TQ = 512
ATTN_TK = 1024
BOUNDED_UNROLL = 8
SCORE_BOUND_LOG2 = 60.0
BF16_ROUNDING_SLACK = 1.02
MLP_TM = 1024
MLP_TF = 1024
VMEM_SMALL = 40 * MIB
VMEM_MEDIUM = 56 * MIB
VMEM_LARGE = 58 * MIB

F32 = jnp.float32
BF16 = jnp.bfloat16


def _resident(block_shape, index_map):
    return pl.BlockSpec(block_shape, index_map, pipeline_mode=pl.Buffered(1))


def _mod_kernel(a_ref, w_ref, b_ref, o_ref):
    a = a_ref[...]
    act = a / (1.0 + jnp.exp(-a))
    o_ref[...] = jnp.dot(act.astype(BF16), w_ref[...].astype(BF16),
                         preferred_element_type=F32) + b_ref[...]


def _modulation(cond, w_mod, b_mod):
    depth = w_mod.shape[0]
    return pl.pallas_call(
        _mod_kernel,
        grid=(depth, 6 * D_MODEL // MOD_TN),
        in_specs=[
            pl.BlockSpec((MOD_ROWS, D_MODEL), lambda l, j: (0, 0)),
            pl.BlockSpec((None, D_MODEL, MOD_TN), lambda l, j: (l, 0, j)),
            pl.BlockSpec((None, 1, MOD_TN), lambda l, j: (l, 0, j)),
        ],
        out_specs=pl.BlockSpec((None, MOD_ROWS, MOD_TN), lambda l, j: (l, 0, j)),
        out_shape=jax.ShapeDtypeStruct((depth, MOD_ROWS, 6 * D_MODEL), F32),
        compiler_params=pltpu.CompilerParams(
            dimension_semantics=("arbitrary", "arbitrary"), vmem_limit_bytes=VMEM_SMALL),
        name="modulation",
    )(cond, w_mod, b_mod.reshape(depth, 1, 6 * D_MODEL))


def _head_rmsnorm(x, g):
    return x * lax.rsqrt(jnp.mean(x * x, axis=-1, keepdims=True) + EPS) * g


def _inproj_kernel(x_ref, mod_ref, g1_ref, w_ref, qg_ref, kg_ref, cos_ref, sa_ref, sb_ref,
                   q_ref, k_ref, vt_ref, pu_ref, cb_ref, z_ref, *, rope):
    x = x_ref[...]
    rinv = lax.rsqrt(jnp.mean(x * x, axis=-1, keepdims=True) + EPS)
    sh = mod_ref[:, 0:D_MODEL]
    gain = g1_ref[...] * (1.0 + mod_ref[:, D_MODEL:2 * D_MODEL])
    h = (x * rinv * gain + sh).astype(BF16)

    def proj(lo, width):
        return jnp.dot(h, w_ref[:, lo:lo + width], preferred_element_type=F32)

    def rotary(t):
        if not rope:
            return t
        return (t * cos_ref[...] + pltpu.roll(t, 3 * HEAD_DIM // 4, axis=1) * sa_ref[...]
                + pltpu.roll(t, HEAD_DIM // 4, axis=1) * sb_ref[...])

    pq = proj(0, ATTN_WIDTH)
    for hh in range(ATTN_HEADS):
        t = _head_rmsnorm(pq[:, hh * HEAD_DIM:(hh + 1) * HEAD_DIM], qg_ref[...])
        q_ref[:, hh * HEAD_DIM:(hh + 1) * HEAD_DIM] = (rotary(t) * Q_SCALE_LOG2).astype(BF16)

    pk = proj(K_OFF, KV_WIDTH)
    for hh in range(KV_HEADS):
        t = rotary(_head_rmsnorm(pk[:, hh * HEAD_DIM:(hh + 1) * HEAD_DIM], kg_ref[...]))
        k_ref[:, hh * HEAD_DIM:(hh + 1) * HEAD_DIM] = t.astype(BF16)

    pv = proj(V_OFF, KV_WIDTH)
    for hh in range(KV_HEADS):
        vt_ref[hh * HEAD_DIM:(hh + 1) * HEAD_DIM, :] = pv[:, hh * HEAD_DIM:(hh + 1) * HEAD_DIM].T.astype(BF16)
    pu_ref[...] = proj(POOL_OFF, POOL_WIDTH)
    cb_ref[...] = proj(CB_OFF, CONV_WIDTH)
    z_ref[...] = proj(CC_OFF, CONV_WIDTH) * proj(CV_OFF, CONV_WIDTH)


def _in_projection(x, mod, g1, w_in, qg, kg, cos, sa, sb, *, tm, rope):
    b, n, _ = x.shape
    assert n % tm == 0
    row = lambda width: pl.BlockSpec((None, tm, width), lambda bi, i: (bi, i, 0))
    tab = pl.BlockSpec((tm, HEAD_DIM), lambda bi, i: (i, 0))
    vec = lambda width: pl.BlockSpec((1, width), lambda bi, i: (0, 0))
    out_shape = (
        jax.ShapeDtypeStruct((b, n, ATTN_WIDTH), BF16),
        jax.ShapeDtypeStruct((b, n, KV_WIDTH), BF16),
        jax.ShapeDtypeStruct((b, KV_WIDTH, n), BF16),
        jax.ShapeDtypeStruct((b, n, POOL_WIDTH), F32),
        jax.ShapeDtypeStruct((b, n, CONV_WIDTH), F32),
        jax.ShapeDtypeStruct((b, n, CONV_WIDTH), F32),
    )
    return pl.pallas_call(
        functools.partial(_inproj_kernel, rope=rope),
        grid=(b, n // tm),
        in_specs=[
            row(D_MODEL),
            pl.BlockSpec((None, 1, 6 * D_MODEL), lambda bi, i: (bi, 0, 0)),
            vec(D_MODEL),
            _resident((None, D_MODEL, IN_WIDTH), lambda bi, i: (0, 0, 0)),
            vec(HEAD_DIM), vec(HEAD_DIM),
            tab, tab, tab,
        ],
        out_specs=(
            row(ATTN_WIDTH),
            row(KV_WIDTH),
            pl.BlockSpec((None, KV_WIDTH, tm), lambda bi, i: (bi, 0, i)),
            row(POOL_WIDTH), row(CONV_WIDTH), row(CONV_WIDTH),
        ),
        out_shape=out_shape,
        compiler_params=pltpu.CompilerParams(
            dimension_semantics=("arbitrary", "arbitrary"), vmem_limit_bytes=VMEM_MEDIUM),
        name="in_projection",
    )(x, mod, g1, w_in, qg, kg, cos, sa, sb)


def _attn_kernel(*refs, tq, tk, n_chunks, n_cast):
    bounded_ref, q_ref, kc_ref, vtc_ref, k_ref, vt_ref = refs[:6]
    cast_src = refs[6:6 + n_cast]
    o_ref = refs[6 + n_cast]
    cast_dst = refs[7 + n_cast:7 + 2 * n_cast]
    s_ref, m_ref, l_ref, acc_ref = refs[7 + 2 * n_cast:]
    for src, dst in zip(cast_src, cast_dst):
        dst[...] = src[...].astype(BF16)

    qs = jnp.concatenate([q_ref[:, g * HEAD_DIM:(g + 1) * HEAD_DIM] for g in range(GQA_GROUP)], axis=0)

    def scores_t(k):
        return lax.dot_general(k, qs, (((1,), (1,)), ((), ())), preferred_element_type=F32)

    def weigh(p, vt):
        return jnp.dot(vt, p.astype(BF16), preferred_element_type=F32), jnp.sum(p, axis=0, keepdims=True)

    def write_out():
        out = (acc_ref[...] / l_ref[...]).T.astype(o_ref.dtype)
        for g in range(GQA_GROUP):
            o_ref[:, g * HEAD_DIM:(g + 1) * HEAD_DIM] = out[g * tq:(g + 1) * tq]

    @pl.when(bounded_ref[0] != 0)
    def _():
        acc_ref[...], l_ref[...] = weigh(jnp.exp2(scores_t(kc_ref[...])), vtc_ref[...])

        def trip(i, carry):
            for u in range(BOUNDED_UNROLL):
                c = i * BOUNDED_UNROLL + u
                num, den = weigh(jnp.exp2(scores_t(k_ref[pl.ds(c * tk, tk), :])), vt_ref[:, pl.ds(c * tk, tk)])
                acc_ref[...] += num
                l_ref[...] += den
            return carry

        if n_chunks:
            lax.fori_loop(0, n_chunks // BOUNDED_UNROLL, trip, 0)
        write_out()

    @pl.when(bounded_ref[0] == 0)
    def _():
        _online_softmax_t(scores_t, weigh, kc_ref, vtc_ref, k_ref, vt_ref, s_ref, m_ref, l_ref, acc_ref,
                          tk=tk, n_chunks=n_chunks)
        write_out()


def _online_softmax_t(scores_t, weigh, kc_ref, vtc_ref, k_ref, vt_ref, s_ref, m_ref, l_ref, acc_ref, *, tk, n_chunks):
    def scores(c, slot):
        s_ref[slot] = scores_t(k_ref[pl.ds(c * tk, tk), :])

    def update(c, slot):
        s = s_ref[slot]
        m = m_ref[...]
        m_new = jnp.maximum(m, jnp.max(s, axis=0, keepdims=True))
        num, den = weigh(jnp.exp2(s - m_new), vt_ref[:, pl.ds(c * tk, tk)])
        alpha = jnp.exp2(m - m_new)
        acc_ref[...] = alpha * acc_ref[...] + num
        l_ref[...] = alpha * l_ref[...] + den
        m_ref[...] = m_new

    sc = scores_t(kc_ref[...])
    if n_chunks:
        scores(0, 0)
    m0 = jnp.max(sc, axis=0, keepdims=True)
    m_ref[...] = m0
    acc_ref[...], l_ref[...] = weigh(jnp.exp2(sc - m0), vtc_ref[...])

    def pair(i, carry):
        c = 2 * i
        scores(c + 1, 1)
        update(c, 0)
        scores(c + 2, 0)
        update(c + 1, 1)
        return carry

    if n_chunks:
        n_pairs = (n_chunks - 1) // 2
        if n_pairs:
            lax.fori_loop(0, n_pairs, pair, 0)
        if n_chunks - 1 - 2 * n_pairs:
            scores(n_chunks - 1, 1)
            update(n_chunks - 2, 0)
            update(n_chunks - 1, 1)
        else:
            update(n_chunks - 1, 0)


def _attention(q, kc, vtc, k, vt, bounded, *, tq, tk, cast=()):
    b, n, _ = q.shape
    m = kc.shape[1]
    has_lat = k is not None
    if not has_lat:
        k, vt = kc, vtc
    nk = k.shape[1]
    grp = GQA_GROUP * HEAD_DIM
    rows = GQA_GROUP * tq
    nq = n // tq
    n_steps = b * KV_HEADS * nq
    assert n % tq == 0
    assert not has_lat or (nk % tk == 0 and (nk // tk) % BOUNDED_UNROLL == 0)
    assert all(w.shape[1] % n_steps == 0 for w, _ in cast)
    step = lambda bi, h, i: (bi * KV_HEADS + h) * nq + i

    def slab_spec(w, layer):
        return pl.BlockSpec((None, w.shape[1] // n_steps, w.shape[2]), lambda bi, h, i: (layer, step(bi, h, i), 0))

    cast_in_specs = [slab_spec(w, layer) for w, layer in cast]
    cast_out_specs = [slab_spec(w, 0) for w, _ in cast]
    cast_out_shapes = [jax.ShapeDtypeStruct((1,) + w.shape[1:], BF16) for w, _ in cast]
    res = _resident if has_lat else pl.BlockSpec
    outs = pl.pallas_call(
        functools.partial(_attn_kernel, tq=tq, tk=tk, n_chunks=nk // tk if has_lat else 0, n_cast=len(cast)),
        grid=(b, KV_HEADS, nq),
        in_specs=[
            pl.BlockSpec(memory_space=pltpu.SMEM),
            pl.BlockSpec((None, tq, grp), lambda bi, h, i: (bi, i, h)),
            pl.BlockSpec((None, m, HEAD_DIM), lambda bi, h, i: (bi, 0, h)),
            pl.BlockSpec((None, HEAD_DIM, m), lambda bi, h, i: (bi, h, 0)),
            res((None, nk, HEAD_DIM), lambda bi, h, i: (bi, 0, h)),
            res((None, HEAD_DIM, nk), lambda bi, h, i: (bi, h, 0)),
        ] + cast_in_specs,
        out_specs=[pl.BlockSpec((None, tq, grp), lambda bi, h, i: (bi, i, h))] + cast_out_specs,
        out_shape=[jax.ShapeDtypeStruct((b, n, ATTN_WIDTH), BF16)] + cast_out_shapes,
        scratch_shapes=[
            pltpu.VMEM((2, tk, rows), F32),
            pltpu.VMEM((1, rows), F32),
            pltpu.VMEM((1, rows), F32),
            pltpu.VMEM((HEAD_DIM, rows), F32),
        ],
        compiler_params=pltpu.CompilerParams(
            dimension_semantics=("arbitrary", "arbitrary", "arbitrary"), vmem_limit_bytes=VMEM_LARGE),
        name="attention",
    )(bounded, q, kc, vtc, k, vt, *[w for w, _ in cast])
    return outs[0], tuple(outs[1:])


def _mixout_kernel(x_ref, att_ref, pu_ref, pu_prev_ref, pu_next_ref, z_ref, z_prev_ref, z_next_ref, cb_ref,
                   mod_ref, g2_ref, wout_ref, wpool_ref, pscale_ref, convw_ref,
                   x1_ref, h2_ref, epu_ref, ez_ref, cat_ref, *, tm, n, n_sub):
    j = pl.program_id(1)
    first = j == 0
    last = j == pl.num_programs(1) - 1

    epu_ref[0:HALO, :] = jnp.where(first, 0.0, pu_prev_ref[...])
    epu_ref[HALO:HALO + tm, :] = pu_ref[...]
    epu_ref[HALO + tm:, :] = jnp.where(last, 0.0, pu_next_ref[...])
    ez_ref[0:HALO, :] = jnp.where(first, 0.0, z_prev_ref[...])
    ez_ref[HALO:HALO + tm, :] = z_ref[...]
    ez_ref[HALO + tm:, :] = jnp.where(last, 0.0, z_next_ref[...])

    cat_ref[:, 0:ATTN_WIDTH] = att_ref[...]

    rs = tm // n_sub
    for r in range(n_sub):
        r0 = r * rs
        pos = (j * tm + r0 + lax.broadcasted_iota(jnp.int32, (rs, 1), 0)).astype(F32)
        for gi, w in enumerate(POOL_WINDOWS):
            lo = w // 2
            hi = w - 1 - lo
            cols = slice(gi * POOL_GROUP, (gi + 1) * POOL_GROUP)
            tot = epu_ref[HALO + r0 - lo:HALO + r0 - lo + rs, cols]
            for o in range(-lo + 1, hi + 1):
                tot = tot + epu_ref[HALO + r0 + o:HALO + r0 + o + rs, cols]
            cnt = jnp.minimum(pos + (hi + 1), float(n)) - jnp.maximum(pos - lo, 0.0)
            d = tot / cnt - pu_ref[r0:r0 + rs, cols]
            yg = jnp.dot(d.astype(BF16), wpool_ref[gi], preferred_element_type=F32) * pscale_ref[:, cols]
            cat_ref[r0:r0 + rs, ATTN_WIDTH + gi * POOL_GROUP:ATTN_WIDTH + (gi + 1) * POOL_GROUP] = yg.astype(BF16)

        conv = (ez_ref[HALO + r0 - 1:HALO + r0 - 1 + rs, :] * convw_ref[0:1, :]
                + z_ref[r0:r0 + rs, :] * convw_ref[1:2, :]
                + ez_ref[HALO + r0 + 1:HALO + r0 + 1 + rs, :] * convw_ref[2:3, :])
        cat_ref[r0:r0 + rs, ATTN_WIDTH + POOL_WIDTH:] = (cb_ref[r0:r0 + rs, :] * conv).astype(BF16)

    gt1 = mod_ref[:, 2 * D_MODEL:3 * D_MODEL]
    sh2 = mod_ref[:, 3 * D_MODEL:4 * D_MODEL]
    gain2 = g2_ref[...] * (1.0 + mod_ref[:, 4 * D_MODEL:5 * D_MODEL])
    for r in range(n_sub):
        rows = slice(r * rs, (r + 1) * rs)
        x1 = x_ref[rows, :] + gt1 * jnp.dot(cat_ref[rows, :], wout_ref[...], preferred_element_type=F32)
        x1_ref[rows, :] = x1
        rinv = lax.rsqrt(jnp.mean(x1 * x1, axis=-1, keepdims=True) + EPS)
        h2_ref[rows, :] = (x1 * rinv * gain2 + sh2).astype(BF16)


def _mix_out(x, att, pu, z, cb, mod, g2, w_out, w_pool, pool_scale, conv_w, *, tm):
    b, n, _ = x.shape
    assert n % tm == 0 and tm % HALO == 0 and tm % max(1, tm // MIX_SUB_ROWS) == 0
    hb = tm // HALO
    nhb = n // HALO
    row = lambda width: pl.BlockSpec((None, tm, width), lambda bi, i: (bi, i, 0))
    prev = lambda width: pl.BlockSpec((None, HALO, width), lambda bi, i: (bi, jnp.maximum(i * hb - 1, 0), 0))
    nxt = lambda width: pl.BlockSpec((None, HALO, width), lambda bi, i: (bi, jnp.minimum((i + 1) * hb, nhb - 1), 0))
    vec = lambda width: pl.BlockSpec((1, width), lambda bi, i: (0, 0))
    return pl.pallas_call(
        functools.partial(_mixout_kernel, tm=tm, n=n, n_sub=max(1, tm // MIX_SUB_ROWS)),
        grid=(b, n // tm),
        in_specs=[
            row(D_MODEL), row(ATTN_WIDTH),
            row(POOL_WIDTH), prev(POOL_WIDTH), nxt(POOL_WIDTH),
            row(CONV_WIDTH), prev(CONV_WIDTH), nxt(CONV_WIDTH),
            row(CONV_WIDTH),
            pl.BlockSpec((None, 1, 6 * D_MODEL), lambda bi, i: (bi, 0, 0)),
            vec(D_MODEL),
            _resident((None, D_MODEL, D_MODEL), lambda bi, i: (0, 0, 0)),
            _resident((None, len(POOL_WINDOWS), POOL_GROUP, POOL_GROUP), lambda bi, i: (0, 0, 0, 0)),
            vec(POOL_WIDTH),
            pl.BlockSpec((CONV_K, CONV_WIDTH), lambda bi, i: (0, 0)),
        ],
        out_specs=(row(D_MODEL), row(D_MODEL)),
        out_shape=(jax.ShapeDtypeStruct((b, n, D_MODEL), F32), jax.ShapeDtypeStruct((b, n, D_MODEL), BF16)),
        scratch_shapes=[
            pltpu.VMEM((tm + 2 * HALO, POOL_WIDTH), F32),
            pltpu.VMEM((tm + 2 * HALO, CONV_WIDTH), F32),
            pltpu.VMEM((tm, D_MODEL), BF16),
        ],
        compiler_params=pltpu.CompilerParams(
            dimension_semantics=("arbitrary", "arbitrary"), vmem_limit_bytes=VMEM_MEDIUM),
        name="mix_out",
    )(x, att, pu, pu, pu, z, z, z, cb, mod, g2, w_out, w_pool, pool_scale, conv_w)


def _mlp_kernel(h_ref, x_hbm, mod_ref, w1_ref, w2_ref, o_ref, xbuf, xsem, *, tm):
    bi = pl.program_id(0)
    i = pl.program_id(1)
    f = pl.program_id(2)
    nf = pl.num_programs(2)

    def residual_copy():
        return pltpu.make_async_copy(x_hbm.at[bi, pl.ds(i * tm, tm), :], xbuf, xsem)

    @pl.when(f == 0)
    def _():
        residual_copy().start()
        o_ref[...] = jnp.zeros(o_ref.shape, F32)

    u = jnp.maximum(jnp.dot(h_ref[...], w1_ref[...], preferred_element_type=F32), 0.0)
    o_ref[...] += jnp.dot((u * u).astype(BF16), w2_ref[...], preferred_element_type=F32)

    @pl.when(f == nf - 1)
    def _():
        residual_copy().wait()
        gt2 = mod_ref[:, 5 * D_MODEL:6 * D_MODEL]
        o_ref[...] = xbuf[...] + gt2 * o_ref[...]


def _mlp(h2, x1, mod, w1, w2, *, tm, tf):
    b, n, _ = x1.shape
    assert n % tm == 0 and D_FF % tf == 0
    return pl.pallas_call(
        functools.partial(_mlp_kernel, tm=tm),
        grid=(b, n // tm, D_FF // tf),
        in_specs=[
            pl.BlockSpec((None, tm, D_MODEL), lambda bi, i, f: (bi, i, 0)),
            pl.BlockSpec(memory_space=pl.ANY),
            pl.BlockSpec((None, 1, 6 * D_MODEL), lambda bi, i, f: (bi, 0, 0)),
            pl.BlockSpec((None, D_MODEL, tf), lambda bi, i, f: (0, 0, f)),
            pl.BlockSpec((None, tf, D_MODEL), lambda bi, i, f: (0, f, 0)),
        ],
        out_specs=pl.BlockSpec((None, tm, D_MODEL), lambda bi, i, f: (bi, i, 0)),
        out_shape=jax.ShapeDtypeStruct((b, n, D_MODEL), F32),
        scratch_shapes=[pltpu.VMEM((tm, D_MODEL), F32), pltpu.SemaphoreType.DMA(())],
        compiler_params=pltpu.CompilerParams(
            dimension_semantics=("arbitrary", "arbitrary", "arbitrary"), vmem_limit_bytes=VMEM_LARGE),
        name="mlp",
    )(h2, x1, mod, w1, w2)


def _rope_tables(n):
    rows = n // GRID_W
    n_freq = HEAD_DIM // 4
    inv = ROPE_BASE ** (-jnp.arange(n_freq, dtype=F32) / n_freq)
    ang_r = jnp.arange(rows, dtype=F32)[:, None] * inv
    ang_c = jnp.arange(GRID_W, dtype=F32)[:, None] * inv

    def table(fn):
        tr = jnp.repeat(fn(ang_r), GRID_W, axis=0)
        tc = jnp.tile(fn(ang_c), (rows, 1))
        return jnp.concatenate([tr, tr, tc, tc], axis=-1)

    cos, sin = table(jnp.cos), table(jnp.sin)
    odd_quarter = (jnp.arange(HEAD_DIM) // (HEAD_DIM // 4)) % 2 == 1
    sa = jnp.where(odd_quarter, 0.0, -sin)
    sb = jnp.where(odd_quarter, sin, 0.0)
    return cos, sa, sb


def kernel(x, c, ctx, c_ctx, w_mod, b_mod, norm1_g, norm2_g, w_in, q_norm_g, k_norm_g, w_pool, pool_scale,
           conv_w, w_out, w_ff1, w_ff2):
    b, n, _ = x.shape
    m = ctx.shape[1]
    depth = w_mod.shape[0]
    assert b + 1 <= MOD_ROWS

    cos, sa, sb = _rope_tables(n)
    cond = jnp.concatenate([c, c_ctx[None, :], jnp.zeros((MOD_ROWS - b - 1, D_MODEL), F32)], axis=0)
    mod_all = _modulation(cond, w_mod, b_mod)

    w_in_b = w_in[0:1].astype(BF16)
    w_pool_b = w_pool.astype(BF16)

    tm_ctx = m
    xc = ctx
    for l in range(depth):
        last = l == depth - 1
        mod = mod_all[l, :b][:, None, :]
        modc = jnp.broadcast_to(mod_all[l, b][None, None, :], (b, 1, 6 * D_MODEL))
        g1 = norm1_g[l][None, :]
        g2 = norm2_g[l][None, :]
        qg = q_norm_g[l][None, :]
        kg = k_norm_g[l][None, :]
        ps = pool_scale[l][None, :]
        wp = w_pool_b[l][None]

        qc, kc, vtc, puc, cbc, zc = _in_projection(xc, modc, g1, w_in_b, qg, kg, cos, sa, sb,
                                                    tm=tm_ctx, rope=False)
        q, k, vt, pu, cb, z = _in_projection(x, mod, g1, w_in_b, qg, kg, cos, sa, sb, tm=ROW_TILE, rope=True)
        cast = [(w_out, l), (w_ff1, l), (w_ff2, l)] + ([] if last else [(w_in, l + 1)])
        score_bound = HEAD_DIM * jnp.max(jnp.abs(qg)) * jnp.max(jnp.abs(kg)) * (Q_SCALE_LOG2 * BF16_ROUNDING_SLACK)
        bounded = (score_bound <= SCORE_BOUND_LOG2).astype(jnp.int32).reshape(1)
        att, w_b = _attention(q, kc, vtc, k, vt, bounded, tq=ATTN_TQ, tk=ATTN_TK, cast=cast)
        w_out_b, w1_b, w2_b = w_b[:3]
        x1, h2 = _mix_out(x, att, pu, z, cb, mod, g2, w_out_b, wp, ps, conv_w[l], tm=ROW_TILE)
        x = _mlp(h2, x1, mod, w1_b, w2_b, tm=MLP_TM, tf=MLP_TF)

        if not last:
            attc, _ = _attention(qc, kc, vtc, None, None, bounded, tq=HEAD_DIM, tk=HEAD_DIM)
            xc1, hc2 = _mix_out(xc, attc, puc, zc, cbc, modc, g2, w_out_b, wp, ps, conv_w[l], tm=tm_ctx)
            xc = _mlp(hc2.reshape(1, b * m, D_MODEL), xc1.reshape(1, b * m, D_MODEL), modc[:1], w1_b, w2_b,
                      tm=b * m, tf=MLP_TF).reshape(b, m, D_MODEL)
            w_in_b = w_b[3]
    return x
```

```python
import functools
import math

import jax
import jax.numpy as jnp
from jax import lax
from jax.experimental import pallas as pl
from jax.experimental.pallas import tpu as pltpu

D_MODEL = 2048
GRID_W = 64
HEAD_DIM = 128
ATTN_WIDTH = D_MODEL // 2
ATTN_HEADS = ATTN_WIDTH // HEAD_DIM
KV_HEADS = ATTN_HEADS // 4
GQA_GROUP = ATTN_HEADS // KV_HEADS
KV_WIDTH = KV_HEADS * HEAD_DIM
POOL_WIDTH = D_MODEL // 4
POOL_WINDOWS = (2, 4, 8, 16)
POOL_GROUP = POOL_WIDTH // len(POOL_WINDOWS)
CONV_WIDTH = D_MODEL // 4
CONV_K = 3
D_FF = 4 * D_MODEL
ROPE_BASE = 10000.0
EPS = 1e-6
ATTN_SCALE = 1.0 / math.sqrt(HEAD_DIM)
Q_SCALE_LOG2 = ATTN_SCALE * math.log2(math.e)

K_OFF = ATTN_WIDTH
V_OFF = K_OFF + KV_WIDTH
POOL_OFF = V_OFF + KV_WIDTH
CB_OFF = POOL_OFF + POOL_WIDTH
CC_OFF = CB_OFF + CONV_WIDTH
CV_OFF = CC_OFF + CONV_WIDTH
IN_WIDTH = CV_OFF + CONV_WIDTH

HALO = 8
MOD_ROWS = 8
MIB = 1024 * 1024

MOD_TN = 1024
ROW_TILE = 512
MIX_SUB_ROWS = 256
ATTN_TQ = 512
ATTN_TK = 1024
BOUNDED_UNROLL = 8
SCORE_BOUND_LOG2 = 60.0
BF16_ROUNDING_SLACK = 1.02
MLP_TM = 1024
MLP_TF = 1024
VMEM_SMALL = 40 * MIB
VMEM_MEDIUM = 56 * MIB
VMEM_LARGE = 58 * MIB

F32 = jnp.float32
BF16 = jnp.bfloat16


def _resident(block_shape, index_map):
    return pl.BlockSpec(block_shape, index_map, pipeline_mode=pl.Buffered(1))


def _mod_kernel(a_ref, w_ref, b_ref, o_ref):
    a = a_ref[...]
    act = a / (1.0 + jnp.exp(-a))
    o_ref[...] = jnp.dot(act.astype(BF16), w_ref[...].astype(BF16),
                         preferred_element_type=F32) + b_ref[...]


def _modulation(cond, w_mod, b_mod):
    depth = w_mod.shape[0]
    return pl.pallas_call(
        _mod_kernel,
        grid=(depth, 6 * D_MODEL // MOD_TN),
        in_specs=[
            pl.BlockSpec((MOD_ROWS, D_MODEL), lambda l, j: (0, 0)),
            pl.BlockSpec((None, D_MODEL, MOD_TN), lambda l, j: (l, 0, j)),
            pl.BlockSpec((None, 1, MOD_TN), lambda l, j: (l, 0, j)),
        ],
        out_specs=pl.BlockSpec((None, MOD_ROWS, MOD_TN), lambda l, j: (l, 0, j)),
        out_shape=jax.ShapeDtypeStruct((depth, MOD_ROWS, 6 * D_MODEL), F32),
        compiler_params=pltpu.CompilerParams(
            dimension_semantics=("arbitrary", "arbitrary"), vmem_limit_bytes=VMEM_SMALL),
        name="modulation",
    )(cond, w_mod, b_mod.reshape(depth, 1, 6 * D_MODEL))


def _head_rmsnorm(x, g):
    return x * lax.rsqrt(jnp.mean(x * x, axis=-1, keepdims=True) + EPS) * g


def _inproj_kernel(x_ref, mod_ref, g1_ref, w_ref, qg_ref, kg_ref, cos_ref, sa_ref, sb_ref,
                   *out_refs, rope, kv_only):
    if kv_only:
        k_ref, vt_ref = out_refs
        k_off, v_off = 0, KV_WIDTH
    else:
        q_ref, k_ref, vt_ref, pu_ref, cb_ref, z_ref = out_refs
        k_off, v_off = K_OFF, V_OFF
    x = x_ref[...]
    rinv = lax.rsqrt(jnp.mean(x * x, axis=-1, keepdims=True) + EPS)
    sh = mod_ref[:, 0:D_MODEL]
    gain = g1_ref[...] * (1.0 + mod_ref[:, D_MODEL:2 * D_MODEL])
    h = (x * rinv * gain + sh).astype(BF16)

    def proj(lo, width):
        return jnp.dot(h, w_ref[:, lo:lo + width], preferred_element_type=F32)

    def rotary(t):
        if not rope:
            return t
        return (t * cos_ref[...] + pltpu.roll(t, 3 * HEAD_DIM // 4, axis=1) * sa_ref[...]
                + pltpu.roll(t, HEAD_DIM // 4, axis=1) * sb_ref[...])

    if not kv_only:
        pq = proj(0, ATTN_WIDTH)
        for hh in range(ATTN_HEADS):
            t = _head_rmsnorm(pq[:, hh * HEAD_DIM:(hh + 1) * HEAD_DIM], qg_ref[...])
            q_ref[:, hh * HEAD_DIM:(hh + 1) * HEAD_DIM] = (rotary(t) * Q_SCALE_LOG2).astype(BF16)

    pk = proj(k_off, KV_WIDTH)
    for hh in range(KV_HEADS):
        t = rotary(_head_rmsnorm(pk[:, hh * HEAD_DIM:(hh + 1) * HEAD_DIM], kg_ref[...]))
        k_ref[:, hh * HEAD_DIM:(hh + 1) * HEAD_DIM] = t.astype(BF16)

    pv = proj(v_off, KV_WIDTH)
    for hh in range(KV_HEADS):
        vt_ref[hh * HEAD_DIM:(hh + 1) * HEAD_DIM, :] = pv[:, hh * HEAD_DIM:(hh + 1) * HEAD_DIM].T.astype(BF16)
    if not kv_only:
        pu_ref[...] = proj(POOL_OFF, POOL_WIDTH)
        cb_ref[...] = proj(CB_OFF, CONV_WIDTH)
        z_ref[...] = proj(CC_OFF, CONV_WIDTH) * proj(CV_OFF, CONV_WIDTH)


def _in_projection(x, mod, g1, w_in, qg, kg, cos, sa, sb, *, tm, rope, kv_only=False):
    b, n, _ = x.shape
    assert n % tm == 0
    row = lambda width: pl.BlockSpec((None, tm, width), lambda bi, i: (bi, i, 0))
    tab = pl.BlockSpec((tm, HEAD_DIM), lambda bi, i: (i, 0))
    vec = lambda width: pl.BlockSpec((1, width), lambda bi, i: (0, 0))
    kv_shapes = (jax.ShapeDtypeStruct((b, n, KV_WIDTH), BF16), jax.ShapeDtypeStruct((b, KV_WIDTH, n), BF16))
    kv_specs = (row(KV_WIDTH), pl.BlockSpec((None, KV_WIDTH, tm), lambda bi, i: (bi, 0, i)))
    if kv_only:
        assert K_OFF % (2 * KV_WIDTH) == 0 and V_OFF == K_OFF + KV_WIDTH
        w_spec = _resident((None, D_MODEL, 2 * KV_WIDTH), lambda bi, i: (0, 0, K_OFF // (2 * KV_WIDTH)))
        out_shape, out_specs = kv_shapes, kv_specs
    else:
        w_spec = _resident((None, D_MODEL, IN_WIDTH), lambda bi, i: (0, 0, 0))
        out_shape = ((jax.ShapeDtypeStruct((b, n, ATTN_WIDTH), BF16),) + kv_shapes
                     + (jax.ShapeDtypeStruct((b, n, POOL_WIDTH), F32),) + 2 * (jax.ShapeDtypeStruct((b, n, CONV_WIDTH), F32),))
        out_specs = (row(ATTN_WIDTH),) + kv_specs + (row(POOL_WIDTH), row(CONV_WIDTH), row(CONV_WIDTH))
    return pl.pallas_call(
        functools.partial(_inproj_kernel, rope=rope, kv_only=kv_only),
        grid=(b, n // tm),
        in_specs=[
            row(D_MODEL),
            pl.BlockSpec((None, 1, 6 * D_MODEL), lambda bi, i: (bi, 0, 0)),
            vec(D_MODEL),
            w_spec,
            vec(HEAD_DIM), vec(HEAD_DIM),
            tab, tab, tab,
        ],
        out_specs=out_specs,
        out_shape=out_shape,
        compiler_params=pltpu.CompilerParams(
            dimension_semantics=("arbitrary", "arbitrary"), vmem_limit_bytes=VMEM_MEDIUM),
        name="in_projection",
    )(x, mod, g1, w_in, qg, kg, cos, sa, sb)


def _attn_kernel(*refs, tq, tk, n_chunks, n_cast):
    bounded_ref, q_ref, kc_ref, vtc_ref, k_ref, vt_ref = refs[:6]
    cast_src = refs[6:6 + n_cast]
    o_ref = refs[6 + n_cast]
    cast_dst = refs[7 + n_cast:7 + 2 * n_cast]
    s_ref, m_ref, l_ref, acc_ref = refs[7 + 2 * n_cast:]
    for src, dst in zip(cast_src, cast_dst):
        dst[...] = src[...].astype(BF16)

    qs = jnp.concatenate([q_ref[:, g * HEAD_DIM:(g + 1) * HEAD_DIM] for g in range(GQA_GROUP)], axis=0)

    def scores_t(k):
        return lax.dot_general(k, qs, (((1,), (1,)), ((), ())), preferred_element_type=F32)

    def weigh(p, vt):
        return jnp.dot(vt, p.astype(BF16), preferred_element_type=F32), jnp.sum(p, axis=0, keepdims=True)

    def write_out():
        out = (acc_ref[...] / l_ref[...]).T.astype(o_ref.dtype)
        for g in range(GQA_GROUP):
            o_ref[:, g * HEAD_DIM:(g + 1) * HEAD_DIM] = out[g * tq:(g + 1) * tq]

    @pl.when(bounded_ref[0] != 0)
    def _():
        acc_ref[...], l_ref[...] = weigh(jnp.exp2(scores_t(kc_ref[...])), vtc_ref[...])

        def trip(i, carry):
            for u in range(BOUNDED_UNROLL):
                c = i * BOUNDED_UNROLL + u
                num, den = weigh(jnp.exp2(scores_t(k_ref[pl.ds(c * tk, tk), :])), vt_ref[:, pl.ds(c * tk, tk)])
                acc_ref[...] += num
                l_ref[...] += den
            return carry

        if n_chunks:
            lax.fori_loop(0, n_chunks // BOUNDED_UNROLL, trip, 0)
        write_out()

    @pl.when(bounded_ref[0] == 0)
    def _():
        _online_softmax_t(scores_t, weigh, kc_ref, vtc_ref, k_ref, vt_ref, s_ref, m_ref, l_ref, acc_ref,
                          tk=tk, n_chunks=n_chunks)
        write_out()


def _online_softmax_t(scores_t, weigh, kc_ref, vtc_ref, k_ref, vt_ref, s_ref, m_ref, l_ref, acc_ref, *, tk, n_chunks):
    def scores(c, slot):
        s_ref[slot] = scores_t(k_ref[pl.ds(c * tk, tk), :])

    def update(c, slot):
        s = s_ref[slot]
        m = m_ref[...]
        m_new = jnp.maximum(m, jnp.max(s, axis=0, keepdims=True))
        num, den = weigh(jnp.exp2(s - m_new), vt_ref[:, pl.ds(c * tk, tk)])
        alpha = jnp.exp2(m - m_new)
        acc_ref[...] = alpha * acc_ref[...] + num
        l_ref[...] = alpha * l_ref[...] + den
        m_ref[...] = m_new

    sc = scores_t(kc_ref[...])
    if n_chunks:
        scores(0, 0)
    m0 = jnp.max(sc, axis=0, keepdims=True)
    m_ref[...] = m0
    acc_ref[...], l_ref[...] = weigh(jnp.exp2(sc - m0), vtc_ref[...])

    def pair(i, carry):
        c = 2 * i
        scores(c + 1, 1)
        update(c, 0)
        scores(c + 2, 0)
        update(c + 1, 1)
        return carry

    if n_chunks:
        n_pairs = (n_chunks - 1) // 2
        if n_pairs:
            lax.fori_loop(0, n_pairs, pair, 0)
        if n_chunks - 1 - 2 * n_pairs:
            scores(n_chunks - 1, 1)
            update(n_chunks - 2, 0)
            update(n_chunks - 1, 1)
        else:
            update(n_chunks - 1, 0)


def _attention(q, kc, vtc, k, vt, bounded, *, tq, tk, cast=()):
    b, n, _ = q.shape
    m = kc.shape[1]
    has_lat = k is not None
    if not has_lat:
        k, vt = kc, vtc
    nk = k.shape[1]
    grp = GQA_GROUP * HEAD_DIM
    rows = GQA_GROUP * tq
    nq = n // tq
    n_steps = b * KV_HEADS * nq
    assert n % tq == 0
    assert not has_lat or (nk % tk == 0 and (nk // tk) % BOUNDED_UNROLL == 0)
    assert all(w.shape[1] % n_steps == 0 for w, _ in cast)
    step = lambda bi, h, i: (bi * KV_HEADS + h) * nq + i

    def slab_spec(w, layer):
        return pl.BlockSpec((None, w.shape[1] // n_steps, w.shape[2]), lambda bi, h, i: (layer, step(bi, h, i), 0))

    cast_in_specs = [slab_spec(w, layer) for w, layer in cast]
    cast_out_specs = [slab_spec(w, 0) for w, _ in cast]
    cast_out_shapes = [jax.ShapeDtypeStruct((1,) + w.shape[1:], BF16) for w, _ in cast]
    res = _resident if has_lat else pl.BlockSpec
    outs = pl.pallas_call(
        functools.partial(_attn_kernel, tq=tq, tk=tk, n_chunks=nk // tk if has_lat else 0, n_cast=len(cast)),
        grid=(b, KV_HEADS, nq),
        in_specs=[
            pl.BlockSpec(memory_space=pltpu.SMEM),
            pl.BlockSpec((None, tq, grp), lambda bi, h, i: (bi, i, h)),
            pl.BlockSpec((None, m, HEAD_DIM), lambda bi, h, i: (bi, 0, h)),
            pl.BlockSpec((None, HEAD_DIM, m), lambda bi, h, i: (bi, h, 0)),
            res((None, nk, HEAD_DIM), lambda bi, h, i: (bi, 0, h)),
            res((None, HEAD_DIM, nk), lambda bi, h, i: (bi, h, 0)),
        ] + cast_in_specs,
        out_specs=[pl.BlockSpec((None, tq, grp), lambda bi, h, i: (bi, i, h))] + cast_out_specs,
        out_shape=[jax.ShapeDtypeStruct((b, n, ATTN_WIDTH), BF16)] + cast_out_shapes,
        scratch_shapes=[
            pltpu.VMEM((2, tk, rows), F32),
            pltpu.VMEM((1, rows), F32),
            pltpu.VMEM((1, rows), F32),
            pltpu.VMEM((HEAD_DIM, rows), F32),
        ],
        compiler_params=pltpu.CompilerParams(
            dimension_semantics=("arbitrary", "arbitrary", "arbitrary"), vmem_limit_bytes=VMEM_LARGE),
        name="attention",
    )(bounded, q, kc, vtc, k, vt, *[w for w, _ in cast])
    return outs[0], tuple(outs[1:])


def _mixout_kernel(x_ref, att_ref, pu_ref, pu_prev_ref, pu_next_ref, z_ref, z_prev_ref, z_next_ref, cb_ref,
                   mod_ref, g2_ref, wout_ref, wpool_ref, pscale_ref, convw_ref,
                   x1_ref, h2_ref, epu_ref, ez_ref, cat_ref, *, tm, n, n_sub):
    j = pl.program_id(1)
    first = j == 0
    last = j == pl.num_programs(1) - 1

    epu_ref[0:HALO, :] = jnp.where(first, 0.0, pu_prev_ref[...])
    epu_ref[HALO:HALO + tm, :] = pu_ref[...]
    epu_ref[HALO + tm:, :] = jnp.where(last, 0.0, pu_next_ref[...])
    ez_ref[0:HALO, :] = jnp.where(first, 0.0, z_prev_ref[...])
    ez_ref[HALO:HALO + tm, :] = z_ref[...]
    ez_ref[HALO + tm:, :] = jnp.where(last, 0.0, z_next_ref[...])

    cat_ref[:, 0:ATTN_WIDTH] = att_ref[...]

    rs = tm // n_sub
    for r in range(n_sub):
        r0 = r * rs
        pos = (j * tm + r0 + lax.broadcasted_iota(jnp.int32, (rs, 1), 0)).astype(F32)
        for gi, w in enumerate(POOL_WINDOWS):
            lo = w // 2
            hi = w - 1 - lo
            cols = slice(gi * POOL_GROUP, (gi + 1) * POOL_GROUP)
            tot = epu_ref[HALO + r0 - lo:HALO + r0 - lo + rs, cols]
            for o in range(-lo + 1, hi + 1):
                tot = tot + epu_ref[HALO + r0 + o:HALO + r0 + o + rs, cols]
            cnt = jnp.minimum(pos + (hi + 1), float(n)) - jnp.maximum(pos - lo, 0.0)
            d = tot / cnt - pu_ref[r0:r0 + rs, cols]
            yg = jnp.dot(d.astype(BF16), wpool_ref[gi], preferred_element_type=F32) * pscale_ref[:, cols]
            cat_ref[r0:r0 + rs, ATTN_WIDTH + gi * POOL_GROUP:ATTN_WIDTH + (gi + 1) * POOL_GROUP] = yg.astype(BF16)

        conv = (ez_ref[HALO + r0 - 1:HALO + r0 - 1 + rs, :] * convw_ref[0:1, :]
                + z_ref[r0:r0 + rs, :] * convw_ref[1:2, :]
                + ez_ref[HALO + r0 + 1:HALO + r0 + 1 + rs, :] * convw_ref[2:3, :])
        cat_ref[r0:r0 + rs, ATTN_WIDTH + POOL_WIDTH:] = (cb_ref[r0:r0 + rs, :] * conv).astype(BF16)

    gt1 = mod_ref[:, 2 * D_MODEL:3 * D_MODEL]
    sh2 = mod_ref[:, 3 * D_MODEL:4 * D_MODEL]
    gain2 = g2_ref[...] * (1.0 + mod_ref[:, 4 * D_MODEL:5 * D_MODEL])
    for r in range(n_sub):
        rows = slice(r * rs, (r + 1) * rs)
        x1 = x_ref[rows, :] + gt1 * jnp.dot(cat_ref[rows, :], wout_ref[...], preferred_element_type=F32)
        x1_ref[rows, :] = x1
        rinv = lax.rsqrt(jnp.mean(x1 * x1, axis=-1, keepdims=True) + EPS)
        h2_ref[rows, :] = (x1 * rinv * gain2 + sh2).astype(BF16)


def _mix_out(x, att, pu, z, cb, mod, g2, w_out, w_pool, pool_scale, conv_w, *, tm):
    b, n, _ = x.shape
    assert n % tm == 0 and tm % HALO == 0 and tm % max(1, tm // MIX_SUB_ROWS) == 0
    hb = tm // HALO
    nhb = n // HALO
    row = lambda width: pl.BlockSpec((None, tm, width), lambda bi, i: (bi, i, 0))
    prev = lambda width: pl.BlockSpec((None, HALO, width), lambda bi, i: (bi, jnp.maximum(i * hb - 1, 0), 0))
    nxt = lambda width: pl.BlockSpec((None, HALO, width), lambda bi, i: (bi, jnp.minimum((i + 1) * hb, nhb - 1), 0))
    vec = lambda width: pl.BlockSpec((1, width), lambda bi, i: (0, 0))
    return pl.pallas_call(
        functools.partial(_mixout_kernel, tm=tm, n=n, n_sub=max(1, tm // MIX_SUB_ROWS)),
        grid=(b, n // tm),
        in_specs=[
            row(D_MODEL), row(ATTN_WIDTH),
            row(POOL_WIDTH), prev(POOL_WIDTH), nxt(POOL_WIDTH),
            row(CONV_WIDTH), prev(CONV_WIDTH), nxt(CONV_WIDTH),
            row(CONV_WIDTH),
            pl.BlockSpec((None, 1, 6 * D_MODEL), lambda bi, i: (bi, 0, 0)),
            vec(D_MODEL),
            _resident((None, D_MODEL, D_MODEL), lambda bi, i: (0, 0, 0)),
            _resident((None, len(POOL_WINDOWS), POOL_GROUP, POOL_GROUP), lambda bi, i: (0, 0, 0, 0)),
            vec(POOL_WIDTH),
            pl.BlockSpec((CONV_K, CONV_WIDTH), lambda bi, i: (0, 0)),
        ],
        out_specs=(row(D_MODEL), row(D_MODEL)),
        out_shape=(jax.ShapeDtypeStruct((b, n, D_MODEL), F32), jax.ShapeDtypeStruct((b, n, D_MODEL), BF16)),
        scratch_shapes=[
            pltpu.VMEM((tm + 2 * HALO, POOL_WIDTH), F32),
            pltpu.VMEM((tm + 2 * HALO, CONV_WIDTH), F32),
            pltpu.VMEM((tm, D_MODEL), BF16),
        ],
        compiler_params=pltpu.CompilerParams(
            dimension_semantics=("arbitrary", "arbitrary"), vmem_limit_bytes=VMEM_MEDIUM),
        name="mix_out",
    )(x, att, pu, pu, pu, z, z, z, cb, mod, g2, w_out, w_pool, pool_scale, conv_w)


def _mlp_kernel(h_ref, x_hbm, mod_ref, w1_ref, w2_ref, o_ref, xbuf, xsem, *, tm):
    bi = pl.program_id(0)
    i = pl.program_id(1)
    f = pl.program_id(2)
    nf = pl.num_programs(2)

    def residual_copy():
        return pltpu.make_async_copy(x_hbm.at[bi, pl.ds(i * tm, tm), :], xbuf, xsem)

    @pl.when(f == 0)
    def _():
        residual_copy().start()
        o_ref[...] = jnp.zeros(o_ref.shape, F32)

    u = jnp.maximum(jnp.dot(h_ref[...], w1_ref[...], preferred_element_type=F32), 0.0)
    o_ref[...] += jnp.dot((u * u).astype(BF16), w2_ref[...], preferred_element_type=F32)

    @pl.when(f == nf - 1)
    def _():
        residual_copy().wait()
        gt2 = mod_ref[:, 5 * D_MODEL:6 * D_MODEL]
        o_ref[...] = xbuf[...] + gt2 * o_ref[...]


def _mlp(h2, x1, mod, w1, w2, *, tm, tf):
    b, n, _ = x1.shape
    assert n % tm == 0 and D_FF % tf == 0
    return pl.pallas_call(
        functools.partial(_mlp_kernel, tm=tm),
        grid=(b, n // tm, D_FF // tf),
        in_specs=[
            pl.BlockSpec((None, tm, D_MODEL), lambda bi, i, f: (bi, i, 0)),
            pl.BlockSpec(memory_space=pl.ANY),
            pl.BlockSpec((None, 1, 6 * D_MODEL), lambda bi, i, f: (bi, 0, 0)),
            pl.BlockSpec((None, D_MODEL, tf), lambda bi, i, f: (0, 0, f)),
            pl.BlockSpec((None, tf, D_MODEL), lambda bi, i, f: (0, f, 0)),
        ],
        out_specs=pl.BlockSpec((None, tm, D_MODEL), lambda bi, i, f: (bi, i, 0)),
        out_shape=jax.ShapeDtypeStruct((b, n, D_MODEL), F32),
        scratch_shapes=[pltpu.VMEM((tm, D_MODEL), F32), pltpu.SemaphoreType.DMA(())],
        compiler_params=pltpu.CompilerParams(
            dimension_semantics=("arbitrary", "arbitrary", "arbitrary"), vmem_limit_bytes=VMEM_LARGE),
        name="mlp",
    )(h2, x1, mod, w1, w2)


def _rope_tables(n):
    rows = n // GRID_W
    n_freq = HEAD_DIM // 4
    inv = ROPE_BASE ** (-jnp.arange(n_freq, dtype=F32) / n_freq)
    ang_r = jnp.arange(rows, dtype=F32)[:, None] * inv
    ang_c = jnp.arange(GRID_W, dtype=F32)[:, None] * inv

    def table(fn):
        tr = jnp.repeat(fn(ang_r), GRID_W, axis=0)
        tc = jnp.tile(fn(ang_c), (rows, 1))
        return jnp.concatenate([tr, tr, tc, tc], axis=-1)

    cos, sin = table(jnp.cos), table(jnp.sin)
    odd_quarter = (jnp.arange(HEAD_DIM) // (HEAD_DIM // 4)) % 2 == 1
    sa = jnp.where(odd_quarter, 0.0, -sin)
    sb = jnp.where(odd_quarter, sin, 0.0)
    return cos, sa, sb


def kernel(x, c, ctx, c_ctx, w_mod, b_mod, norm1_g, norm2_g, w_in, q_norm_g, k_norm_g, w_pool, pool_scale,
           conv_w, w_out, w_ff1, w_ff2):
    b, n, _ = x.shape
    m = ctx.shape[1]
    depth = w_mod.shape[0]
    assert b + 1 <= MOD_ROWS

    cos, sa, sb = _rope_tables(n)
    cond = jnp.concatenate([c, c_ctx[None, :], jnp.zeros((MOD_ROWS - b - 1, D_MODEL), F32)], axis=0)
    mod_all = _modulation(cond, w_mod, b_mod)

    w_in_b = w_in[0:1].astype(BF16)
    w_pool_b = w_pool.astype(BF16)

    tm_ctx = m
    xc = ctx
    for l in range(depth):
        last = l == depth - 1
        mod = mod_all[l, :b][:, None, :]
        modc = jnp.broadcast_to(mod_all[l, b][None, None, :], (b, 1, 6 * D_MODEL))
        g1 = norm1_g[l][None, :]
        g2 = norm2_g[l][None, :]
        qg = q_norm_g[l][None, :]
        kg = k_norm_g[l][None, :]
        ps = pool_scale[l][None, :]
        wp = w_pool_b[l][None]

        if last:
            kc, vtc = _in_projection(xc, modc, g1, w_in_b, qg, kg, cos, sa, sb, tm=tm_ctx, rope=False, kv_only=True)
        else:
            qc, kc, vtc, puc, cbc, zc = _in_projection(xc, modc, g1, w_in_b, qg, kg, cos, sa, sb,
                                                        tm=tm_ctx, rope=False)
        q, k, vt, pu, cb, z = _in_projection(x, mod, g1, w_in_b, qg, kg, cos, sa, sb, tm=ROW_TILE, rope=True)
        cast = [(w_out, l), (w_ff1, l), (w_ff2, l)] + ([] if last else [(w_in, l + 1)])
        score_bound = HEAD_DIM * jnp.max(jnp.abs(qg)) * jnp.max(jnp.abs(kg)) * (Q_SCALE_LOG2 * BF16_ROUNDING_SLACK)
        bounded = (score_bound <= SCORE_BOUND_LOG2).astype(jnp.int32).reshape(1)
        att, w_b = _attention(q, kc, vtc, k, vt, bounded, tq=ATTN_TQ, tk=ATTN_TK, cast=cast)
        w_out_b, w1_b, w2_b = w_b[:3]
        x1, h2 = _mix_out(x, att, pu, z, cb, mod, g2, w_out_b, wp, ps, conv_w[l], tm=ROW_TILE)
        x = _mlp(h2, x1, mod, w1_b, w2_b, tm=MLP_TM, tf=MLP_TF)

        if not last:
            attc, _ = _attention(qc, kc, vtc, None, None, bounded, tq=HEAD_DIM, tk=HEAD_DIM)
            xc1, hc2 = _mix_out(xc, attc, puc, zc, cbc, modc, g2, w_out_b, wp, ps, conv_w[l], tm=tm_ctx)
            xc = _mlp(hc2.reshape(1, b * m, D_MODEL), xc1.reshape(1, b * m, D_MODEL), modc[:1], w1_b, w2_b,
                      tm=b * m, tf=MLP_TF).reshape(b, m, D_MODEL)
            w_in_b = w_b[3]
    return x
```
